```python
import math
import jax
import jax.numpy as jnp
from jax import lax
import numpy as np


D_MODEL = 1024
BATCH = 1
SEQ = 16384
DEPTH = 2

CHUNK = 64
NORM_EPS = 1e-6
BRANCH_WIDTH = D_MODEL // 2
N_BRANCH = 3
GLA_HEADS = 4
GLA_DK = BRANCH_WIDTH // GLA_HEADS
GLA_DV = BRANCH_WIDTH // GLA_HEADS
GLA_RANK = 16
GLA_GATE_NORM = 16.0
S5_GROUP_CH = 16
S5_GROUPS = BRANCH_WIDTH // S5_GROUP_CH
S5_STATE = 64
S5_WIDTH = S5_GROUPS * S5_GROUP_CH
ML_HEADS = 4
ML_DH = BRANCH_WIDTH // ML_HEADS
ML_CONV = 4
D_FF = 2816
FFN_CONV = 3

GLA_QK = GLA_HEADS * GLA_DK
GLA_V = GLA_HEADS * GLA_DV
ML_W = ML_HEADS * ML_DH
IN_SPLITS = (GLA_QK, GLA_QK, GLA_V, GLA_RANK, GLA_V, S5_WIDTH,
             ML_W, ML_W, ML_W, ML_HEADS, ML_HEADS, ML_W, N_BRANCH * D_MODEL)
D_IN = sum(IN_SPLITS)
SPLIT_POINTS = tuple(int(v) for v in np.cumsum(IN_SPLITS)[:-1])

kernel_name = "hybrid_gla_s5_mlstm_convffn"


def rms_norm(x, gain):
    xf = x.astype(jnp.float32)
    y = xf * lax.rsqrt(jnp.mean(xf * xf, axis=-1, keepdims=True) + NORM_EPS)
    return (y * gain.astype(jnp.float32)).astype(x.dtype)


def causal_dwconv(x, w, b):
    k_width, ch = w.shape
    y = lax.conv_general_dilated(x, w.astype(x.dtype)[:, None, :], window_strides=(1,),
                                 padding=[(k_width - 1, 0)],
                                 dimension_numbers=('NWC', 'WIO', 'NWC'),
                                 feature_group_count=ch)
    return y + b.astype(x.dtype)


def _heads(t, n):
    return t.reshape(t.shape[0], t.shape[1], n, -1)


def _to_chunks(t):
    bsz, s, h, d = t.shape
    return t.reshape(bsz, s // CHUNK, CHUNK, h, d).transpose(0, 3, 1, 2, 4)


def _from_chunks(t):
    bsz, h, n, c, d = t.shape
    return t.transpose(0, 2, 3, 1, 4).reshape(bsz, n * c, h, d)


def gla_chunked(q, k, v, log_a):
    f32 = jnp.float32
    q = _to_chunks(q.astype(f32)) * GLA_DK ** -0.5
    k = _to_chunks(k.astype(f32))
    v = _to_chunks(v.astype(f32))
    b = jnp.cumsum(_to_chunks(log_a.astype(f32)), axis=3)
    b_last = b[:, :, :, -1:, :]
    q_dec = q * jnp.exp(b)
    k_inv = k * jnp.exp(-b)
    kv = jnp.einsum('bhnld,bhnle->bhnde', k * jnp.exp(b_last - b), v)
    decay = jnp.exp(b_last[:, :, :, 0, :])

    def step(state, inp):
        dec, kv_c = inp
        return dec[..., None] * state + kv_c, state

    s0 = jnp.zeros(kv.shape[:2] + kv.shape[3:], f32)
    _, s_prev = lax.scan(step, s0, (jnp.moveaxis(decay, 2, 0), jnp.moveaxis(kv, 2, 0)))
    s_prev = jnp.moveaxis(s_prev, 0, 2)
    causal = jnp.tril(jnp.ones((CHUNK, CHUNK), bool))
    scores = jnp.where(causal, jnp.einsum('bhnld,bhnmd->bhnlm', q_dec, k_inv), 0.0)
    o = (jnp.einsum('bhnld,bhnde->bhnle', q_dec, s_prev)
         + jnp.einsum('bhnlm,bhnme->bhnle', scores, v))
    return _from_chunks(o)


def s5_scan(u, a_re, a_im, log_dt, b_re, b_im, c_re, c_im, d):
    f32 = jnp.float32
    bsz, s, _ = u.shape
    ug = u.astype(f32).reshape(bsz, s, S5_GROUPS, S5_GROUP_CH)
    a_re = a_re.astype(f32)
    a_im = a_im.astype(f32)
    dt = jnp.exp(log_dt.astype(f32))[:, None]
    mag = jnp.exp(dt * a_re)
    ab_re = mag * jnp.cos(dt * a_im)
    ab_im = mag * jnp.sin(dt * a_im)
    den = a_re * a_re + a_im * a_im
    zr = ab_re - 1.0
    f_re = (zr * a_re + ab_im * a_im) / den
    f_im = (ab_im * a_re - zr * a_im) / den
    b_re = b_re.astype(f32)
    b_im = b_im.astype(f32)
    bb_re = f_re[..., None] * b_re - f_im[..., None] * b_im
    bb_im = f_re[..., None] * b_im + f_im[..., None] * b_re
    bu_re = jnp.einsum('bsgh,gph->bsgp', ug, bb_re)
    bu_im = jnp.einsum('bsgh,gph->bsgp', ug, bb_im)
    ar = jnp.broadcast_to(ab_re, bu_re.shape)
    ai = jnp.broadcast_to(ab_im, bu_re.shape)

    def combine(e1, e2):
        a1r, a1i, b1r, b1i = e1
        a2r, a2i, b2r, b2i = e2
        return (a2r * a1r - a2i * a1i, a2r * a1i + a2i * a1r,
                a2r * b1r - a2i * b1i + b2r, a2r * b1i + a2i * b1r + b2i)

    _, _, xr, xi = lax.associative_scan(combine, (ar, ai, bu_re, bu_im), axis=1)
    y = (jnp.einsum('bsgp,ghp->bsgh', xr, c_re.astype(f32))
         - jnp.einsum('bsgp,ghp->bsgh', xi, c_im.astype(f32))
         + d.astype(f32) * ug)
    return y.reshape(bsz, s, S5_WIDTH)


def mlstm_chunked(q, k, v, i_pre, f_pre):
    f32 = jnp.float32
    bsz, s, h, _ = q.shape
    n_chunks = s // CHUNK
    q = _to_chunks(q.astype(f32))
    k = _to_chunks(k.astype(f32)) * ML_DH ** -0.5
    v = _to_chunks(v.astype(f32))

    def gate_chunks(t):
        return t.astype(f32).reshape(bsz, n_chunks, CHUNK, h).transpose(0, 3, 1, 2)

    log_i = gate_chunks(i_pre)
    b = jnp.cumsum(jax.nn.log_sigmoid(gate_chunks(f_pre)), axis=-1)
    b_last = b[..., -1]
    g_end = b_last[..., None] - b + log_i

    def step(carry, inp):
        c_st, n_st, m_st = carry
        k_c, v_c, g_c, bl = inp
        m_new = jnp.maximum(bl + m_st, jnp.max(g_c, axis=-1))
        w = jnp.exp(g_c - m_new[..., None])
        sc = jnp.exp(bl + m_st - m_new)
        c_new = sc[..., None, None] * c_st + jnp.einsum('bhl,bhld,bhle->bhde', w, k_c, v_c)
        n_new = sc[..., None] * n_st + jnp.einsum('bhl,bhld->bhd', w, k_c)
        return (c_new, n_new, m_new), (c_st, n_st, m_st)

    init = (jnp.zeros((bsz, h, ML_DH, ML_DH), f32), jnp.zeros((bsz, h, ML_DH), f32),
            jnp.zeros((bsz, h), f32))
    xs = (jnp.moveaxis(k, 2, 0), jnp.moveaxis(v, 2, 0), jnp.moveaxis(g_end, 2, 0),
          jnp.moveaxis(b_last, 2, 0))
    _, (c_prev, n_prev, m_prev) = lax.scan(step, init, xs)
    c_prev = jnp.moveaxis(c_prev, 0, 2)
    n_prev = jnp.moveaxis(n_prev, 0, 2)
    m_prev = jnp.moveaxis(m_prev, 0, 2)
    causal = jnp.tril(jnp.ones((CHUNK, CHUNK), bool))
    log_w = jnp.where(causal, b[..., :, None] - b[..., None, :] + log_i[..., None, :], -jnp.inf)
    m_inter = b + m_prev[..., None]
    m = jnp.maximum(m_inter, jnp.max(log_w, axis=-1))
    s_inter = jnp.exp(m_inter - m)
    qk = jnp.einsum('bhnld,bhnmd->bhnlm', q, k) * jnp.exp(log_w - m[..., None])
    num = (s_inter[..., None] * jnp.einsum('bhnld,bhnde->bhnle', q, c_prev)
           + jnp.einsum('bhnlm,bhnme->bhnle', qk, v))
    den = s_inter * jnp.einsum('bhnld,bhnd->bhnl', q, n_prev) + jnp.sum(qk, axis=-1)
    hid = num / jnp.maximum(jnp.abs(den), jnp.exp(-m))[..., None]
    return _from_chunks(hid)


def hybrid_mixer(u, w_in, gla_w_gk, gla_b_gk, gla_norm, gla_w_proj,
                 s5_a_re, s5_a_im, s5_log_dt, s5_b_re, s5_b_im, s5_c_re, s5_c_im, s5_d,
                 s5_w_glu, s5_b_glu, s5_w_proj, ml_conv_w, ml_conv_b, ml_b_i, ml_b_f,
                 ml_w_proj, w_out):
    bsz, s, _ = u.shape
    z = u @ w_in
    (g_q, g_k, g_v, g_lr, g_r, s_u, m_q, m_k, m_v, m_i, m_f, m_o,
     gate_pre) = jnp.split(z, SPLIT_POINTS, axis=-1)
    log_a = jax.nn.log_sigmoid((g_lr @ gla_w_gk + gla_b_gk).astype(jnp.float32)) / GLA_GATE_NORM
    o = gla_chunked(_heads(g_q, GLA_HEADS), _heads(g_k, GLA_HEADS), _heads(g_v, GLA_HEADS),
                    _heads(log_a, GLA_HEADS))
    o = rms_norm(o, gla_norm) * jax.nn.silu(_heads(g_r, GLA_HEADS).astype(jnp.float32))
    y_gla = o.reshape(bsz, s, GLA_V).astype(u.dtype) @ gla_w_proj
    y = jax.nn.gelu(s5_scan(s_u, s5_a_re, s5_a_im, s5_log_dt, s5_b_re, s5_b_im,
                            s5_c_re, s5_c_im, s5_d)).astype(u.dtype)
    y = y * jax.nn.sigmoid(y @ s5_w_glu + s5_b_glu)
    y_s5 = y @ s5_w_proj
    qk = jax.nn.silu(causal_dwconv(jnp.concatenate([m_q, m_k], axis=-1), ml_conv_w, ml_conv_b))
    q_ml, k_ml = jnp.split(qk, 2, axis=-1)
    hid = mlstm_chunked(_heads(q_ml, ML_HEADS), _heads(k_ml, ML_HEADS), _heads(m_v, ML_HEADS),
                        m_i + ml_b_i, m_f + ml_b_f)
    hid = jax.nn.sigmoid(_heads(m_o, ML_HEADS).astype(jnp.float32)) * hid
    y_ml = hid.reshape(bsz, s, ML_W).astype(u.dtype) @ ml_w_proj
    gates = jax.nn.sigmoid(gate_pre).reshape(bsz, s, N_BRANCH, D_MODEL)
    merged = gates[:, :, 0] * y_gla + gates[:, :, 1] * y_s5 + gates[:, :, 2] * y_ml
    return merged @ w_out


def conv_ffn(u, w_up, w_gate, conv_w, conv_b, w_down):
    a = causal_dwconv(u @ w_up, conv_w, conv_b)
    return (jax.nn.gelu(a) * (u @ w_gate)) @ w_down


def setup_inputs(seed: int = 0) -> dict:
    key = jax.random.key(seed)
    ks = iter(list(jax.random.split(key, 40)))
    L = DEPTH
    f32 = jnp.float32

    def nrm(shape, scale):
        return scale * jax.random.normal(next(ks), shape, f32)

    def gain(dim):
        return 1.0 + nrm((L, dim), 0.02)

    n_idx = jnp.arange(S5_STATE, dtype=f32)
    return {
        'x': nrm((BATCH, SEQ, D_MODEL), 1.0),
        'norm_mix_pre': gain(D_MODEL),
        'norm_mix_post': gain(D_MODEL),
        'norm_ffn_pre': gain(D_MODEL),
        'norm_ffn_post': gain(D_MODEL),
        'w_in': nrm((L, D_MODEL, D_IN), D_MODEL ** -0.5),
        'gla_w_gk': nrm((L, GLA_RANK, GLA_QK), GLA_RANK ** -0.5),
        'gla_b_gk': nrm((L, GLA_QK), 0.01),
        'gla_norm': gain(GLA_DV),
        'gla_w_proj': nrm((L, GLA_V, D_MODEL), GLA_V ** -0.5),
        's5_a_re': -0.5 + nrm((L, S5_GROUPS, S5_STATE), 0.01),
        's5_a_im': math.pi * n_idx + nrm((L, S5_GROUPS, S5_STATE), 0.01),
        's5_log_dt': jax.random.uniform(next(ks), (L, S5_GROUPS), f32,
                                        minval=math.log(1e-3), maxval=math.log(1e-1)),
        's5_b_re': nrm((L, S5_GROUPS, S5_STATE, S5_GROUP_CH), S5_GROUP_CH ** -0.5),
        's5_b_im': nrm((L, S5_GROUPS, S5_STATE, S5_GROUP_CH), S5_GROUP_CH ** -0.5),
        's5_c_re': nrm((L, S5_GROUPS, S5_GROUP_CH, S5_STATE), S5_STATE ** -0.5),
        's5_c_im': nrm((L, S5_GROUPS, S5_GROUP_CH, S5_STATE), S5_STATE ** -0.5),
        's5_d': nrm((L, S5_GROUPS, S5_GROUP_CH), 1.0),
        's5_w_glu': nrm((L, S5_WIDTH, S5_WIDTH), S5_WIDTH ** -0.5),
        's5_b_glu': nrm((L, S5_WIDTH), 0.01),
        's5_w_proj': nrm((L, S5_WIDTH, D_MODEL), S5_WIDTH ** -0.5),
        'ml_conv_w': nrm((L, ML_CONV, 2 * ML_W), ML_CONV ** -0.5),
        'ml_conv_b': nrm((L, 2 * ML_W), 0.01),
        'ml_b_i': nrm((L, ML_HEADS), 0.1),
        'ml_b_f': jnp.linspace(3.0, 6.0, ML_HEADS, dtype=f32) + nrm((L, ML_HEADS), 0.1),
        'ml_w_proj': nrm((L, ML_W, D_MODEL), ML_W ** -0.5),
        'w_out': nrm((L, D_MODEL, D_MODEL), D_MODEL ** -0.5),
        'ffn_w_up': nrm((L, D_MODEL, D_FF), D_MODEL ** -0.5),
        'ffn_w_gate': nrm((L, D_MODEL, D_FF), D_MODEL ** -0.5),
        'ffn_conv_w': nrm((L, FFN_CONV, D_FF), FFN_CONV ** -0.5),
        'ffn_conv_b': nrm((L, D_FF), 0.01),
        'ffn_w_down': nrm((L, D_FF, D_MODEL), D_FF ** -0.5),
    }


def reference(x, norm_mix_pre, norm_mix_post, norm_ffn_pre, norm_ffn_post, w_in,
              gla_w_gk, gla_b_gk, gla_norm, gla_w_proj,
              s5_a_re, s5_a_im, s5_log_dt, s5_b_re, s5_b_im, s5_c_re, s5_c_im, s5_d,
              s5_w_glu, s5_b_glu, s5_w_proj,
              ml_conv_w, ml_conv_b, ml_b_i, ml_b_f, ml_w_proj, w_out,
              ffn_w_up, ffn_w_gate, ffn_conv_w, ffn_conv_b, ffn_w_down):
    for l in range(DEPTH):
        u = rms_norm(x, norm_mix_pre[l])
        mix = hybrid_mixer(u, w_in[l], gla_w_gk[l], gla_b_gk[l], gla_norm[l], gla_w_proj[l],
                           s5_a_re[l], s5_a_im[l], s5_log_dt[l], s5_b_re[l], s5_b_im[l],
                           s5_c_re[l], s5_c_im[l], s5_d[l], s5_w_glu[l], s5_b_glu[l],
                           s5_w_proj[l], ml_conv_w[l], ml_conv_b[l], ml_b_i[l], ml_b_f[l],
                           ml_w_proj[l], w_out[l])
        x = x + rms_norm(mix, norm_mix_post[l])
        u = rms_norm(x, norm_ffn_pre[l])
        ffn = conv_ffn(u, ffn_w_up[l], ffn_w_gate[l], ffn_conv_w[l], ffn_conv_b[l], ffn_w_down[l])
        x = x + rms_norm(ffn, norm_ffn_post[l])
    return x
```

```python
import functools
import math

import jax
import jax.numpy as jnp
from jax import lax
from jax.experimental import pallas as pl
from jax.experimental.pallas import tpu as pltpu

F32 = jnp.float32
BF16 = jnp.bfloat16

D_MODEL = 1024
CHUNK = 64
NORM_EPS = 1e-6
BRANCH = 512
HEADS = 4
HEAD_DIM = BRANCH // HEADS
GLA_RANK = 16
GLA_GATE_NORM = 16.0
S5_GROUP_CH = 16
S5_GROUPS = BRANCH // S5_GROUP_CH
S5_STATE = 64
S5_BLOCK = 128
ML_CONV = 4
D_FF = 2816
FFN_CONV = 3
LANES = 128
SUBLANES = 8

VMEM_LIMIT = 56 * 1024 * 1024


def _rms_norm(x, gain):
    return x * lax.rsqrt(jnp.mean(x * x, axis=-1, keepdims=True) + NORM_EPS) * gain


def _dot(a, b):
    return jnp.dot(a, b, preferred_element_type=F32)


def _dot_nt(a, b):
    return lax.dot_general(a, b, (((1,), (1,)), ((), ())), preferred_element_type=F32)


def _dot_tn(a, b):
    return lax.dot_general(a, b, (((0,), (0,)), ((), ())), preferred_element_type=F32)


def _split3(a):
    hi = a.astype(BF16)
    r1 = a - hi.astype(F32)
    mid = r1.astype(BF16)
    lo = (r1 - mid.astype(F32)).astype(BF16)
    return hi, mid, lo


def _cumsum_rows(tri, a):
    hi, mid, lo = _split3(a)
    return _dot(tri, hi) + _dot(tri, mid) + _dot(tri, lo)


def _log_sigmoid(x):
    return jnp.minimum(x, 0.0) - jnp.log1p(jnp.exp(-jnp.abs(x)))


def _sigmoid(x):
    return 1.0 / (1.0 + jnp.exp(-x))


def _silu(x):
    return x * _sigmoid(x)


def _gelu_tanh(x):
    c = math.sqrt(2.0 / math.pi)
    return x * (0.5 * (1.0 + jnp.tanh(c * (x + 0.044715 * (x * x * x)))))


def _full_spec(shape):
    zeros = (0,) * len(shape)
    return pl.BlockSpec(shape, lambda *_: zeros, pipeline_mode=pl.Buffered(1))


def _s5_in_kernel(x_ref, gain_ref, w_ref, o_ref):
    u = _rms_norm(x_ref[...], gain_ref[...]).astype(BF16)
    o_ref[...] = _dot_nt(w_ref[...], u)


def _s5_in_proj(x, gain, w_t, tile):
    s = x.shape[0]
    return pl.pallas_call(
        _s5_in_kernel,
        grid=(s // tile,),
        in_specs=[pl.BlockSpec((tile, D_MODEL), lambda i: (i, 0)),
                  _full_spec((1, D_MODEL)),
                  _full_spec((BRANCH, D_MODEL))],
        out_specs=pl.BlockSpec((BRANCH, tile), lambda i: (0, i)),
        out_shape=jax.ShapeDtypeStruct((BRANCH, s), F32),
        compiler_params=pltpu.CompilerParams(dimension_semantics=("arbitrary",),
                                             vmem_limit_bytes=VMEM_LIMIT),
        name="s5_in_proj",
    )(x, gain, w_t)


def _cmul(xr, xi, yr, yi):
    return xr * yr - xi * yi, xr * yi + xi * yr


def _s5_kernel(u_ref, are_c_ref, aim_c_ref, are_r_ref, aim_r_ref, ldt_ref,
               bt_re_ref, bt_im_ref, c_re_ref, c_im_ref, ct_re_ref, ct_im_ref, d_ref,
               o_ref, k_sc, t_sc, win_sc, m_sc):
    nch, nblk, blk = u_ref.shape
    half = S5_STATE
    dt = jnp.exp(ldt_ref[...])

    def discretize(are, aim):
        rho, th = dt * are, dt * aim
        mag = jnp.exp(rho)
        abr, abi = mag * jnp.cos(th), mag * jnp.sin(th)
        den = are * are + aim * aim
        zr = abr - 1.0
        fr = (zr * are + abi * aim) / den
        fi = (abi * are - zr * aim) / den
        return rho, th, abr, abi, fr, fi

    rho_c, th_c, abr_c, abi_c, _, _ = discretize(are_c_ref[...], aim_c_ref[...])
    rho_r, th_r, _, _, fr_r, fi_r = discretize(are_r_ref[...], aim_r_ref[...])

    bbr, bbi = _cmul(fr_r, fi_r, bt_re_ref[...], bt_im_ref[...])

    lag_l = lax.broadcasted_iota(jnp.int32, (half, blk), 1).astype(F32)
    pmag = jnp.exp(rho_c * lag_l)
    p_re, p_im = pmag * jnp.cos(th_c * lag_l), pmag * jnp.sin(th_c * lag_l)

    c_re, c_im = c_re_ref[...], c_im_ref[...]
    cb_rows = []
    for hi in range(nch):
        cbr, cbi = _cmul(c_re, c_im, bbr[hi:hi + 1, :], bbi[hi:hi + 1, :])
        cb_rows.append(jnp.concatenate([cbr, cbi], axis=1))
    cb = jnp.concatenate(cb_rows, axis=0)
    pstack = jnp.concatenate([p_re, -p_im], axis=0)
    kern = jnp.dot(cb, pstack, preferred_element_type=F32, precision=lax.Precision.HIGHEST)
    rr = lax.broadcasted_iota(jnp.int32, (nch * nch, blk), 0)
    ll = lax.broadcasted_iota(jnp.int32, (nch * nch, blk), 1)
    d_col = jnp.concatenate([d_ref[...]] * nch, axis=0)
    diag = ((rr >> 4) == (rr & (nch - 1))) & (ll == 0)
    k_sc[...] = kern + jnp.where(diag, d_col, 0.0)

    srow = lax.broadcasted_iota(jnp.int32, (blk, blk), 0)
    lcol = lax.broadcasted_iota(jnp.int32, (blk, blk), 1)
    lower = lcol >= srow

    def build_t(hi, carry):
        r0 = pl.multiple_of(hi * blk, blk)
        for ho in range(nch):
            krow = k_sc[pl.ds(hi * nch + ho, 1), :]
            tile = pltpu.roll(jnp.broadcast_to(krow, (blk, blk)), 0, 1, stride=1, stride_axis=0)
            t_sc[pl.ds(r0, blk), ho * blk:(ho + 1) * blk] = jnp.where(lower, tile, 0.0).astype(BF16)
        return carry

    lax.fori_loop(0, nch, build_t, 0)

    lag_s = (blk - 1 - lax.broadcasted_iota(jnp.int32, (blk, half), 0)).astype(F32)
    qmag = jnp.exp(rho_r * lag_s)
    q_re, q_im = qmag * jnp.cos(th_r * lag_s), qmag * jnp.sin(th_r * lag_s)
    for hi in range(nch):
        wr, wi = _cmul(q_re, q_im, bbr[hi:hi + 1, :], bbi[hi:hi + 1, :])
        win_sc[hi * blk:(hi + 1) * blk, :] = jnp.concatenate([wr, wi], axis=1).astype(BF16)

    p1_re, p1_im = _cmul(p_re, p_im, abr_c, abi_c)
    ct_re, ct_im = ct_re_ref[...], ct_im_ref[...]
    for ho in range(nch):
        mr, mi = _cmul(p1_re, p1_im, ct_re[:, ho:ho + 1], ct_im[:, ho:ho + 1])
        m_sc[:, ho * blk:(ho + 1) * blk] = jnp.concatenate([mr, -mi], axis=0).astype(BF16)

    ucat = jnp.concatenate([u_ref[h] for h in range(nch)], axis=1).astype(BF16)

    x = _dot(ucat, win_sc[...])
    blk_f = float(blk)
    wmag = jnp.exp(rho_r * blk_f)
    w_re, w_im = wmag * jnp.cos(th_r * blk_f), wmag * jnp.sin(th_r * blk_f)
    row = lax.broadcasted_iota(jnp.int32, (nblk, 2 * half), 0)

    def shift_rows(a, k):
        return jnp.where(row >= k, pltpu.roll(a, k, 0), 0.0)

    def state_mul(a, wr, wi):
        return (a * jnp.concatenate([wr, wr], axis=1)
                + pltpu.roll(a, half, 1) * jnp.concatenate([-wi, wi], axis=1))

    step = 1
    while step < nblk:
        x = x + state_mul(shift_rows(x, step), w_re, w_im)
        w_re, w_im = _cmul(w_re, w_im, w_re, w_im)
        step *= 2
    x_start = shift_rows(x, 1).astype(BF16)

    y = _dot(ucat, t_sc[...]) + _dot(x_start, m_sc[...])
    for ho in range(nch):
        o_ref[ho] = y[:, ho * blk:(ho + 1) * blk]


def _s5_core(u_t, prm):
    s = u_t.shape[1]
    nblk = s // S5_BLOCK
    u4 = u_t.reshape(S5_GROUPS, S5_GROUP_CH, nblk, S5_BLOCK)

    def gspec(*dims):
        zeros = (0,) * len(dims)
        return pl.BlockSpec((None,) + dims, lambda g: (g,) + zeros)

    width = S5_GROUP_CH * S5_BLOCK
    out = pl.pallas_call(
        _s5_kernel,
        grid=(S5_GROUPS,),
        in_specs=[gspec(S5_GROUP_CH, nblk, S5_BLOCK),
                  gspec(S5_STATE, 1), gspec(S5_STATE, 1), gspec(1, S5_STATE), gspec(1, S5_STATE),
                  gspec(1, 1),
                  gspec(S5_GROUP_CH, S5_STATE), gspec(S5_GROUP_CH, S5_STATE),
                  gspec(S5_GROUP_CH, S5_STATE), gspec(S5_GROUP_CH, S5_STATE),
                  gspec(S5_STATE, S5_GROUP_CH), gspec(S5_STATE, S5_GROUP_CH),
                  gspec(S5_GROUP_CH, 1)],
        out_specs=gspec(S5_GROUP_CH, nblk, S5_BLOCK),
        out_shape=jax.ShapeDtypeStruct(u4.shape, F32),
        scratch_shapes=[pltpu.VMEM((S5_GROUP_CH * S5_GROUP_CH, S5_BLOCK), F32),
                        pltpu.VMEM((width, width), BF16),
                        pltpu.VMEM((width, 2 * S5_STATE), BF16),
                        pltpu.VMEM((2 * S5_STATE, width), BF16)],
        compiler_params=pltpu.CompilerParams(dimension_semantics=("arbitrary",),
                                             vmem_limit_bytes=VMEM_LIMIT),
        name="s5_core",
    )(u4, *prm)
    return out.reshape(BRANCH, s)


def _s5_params(a_re, a_im, log_dt, b_re, b_im, c_re, c_im, d):
    g = S5_GROUPS
    return (a_re.reshape(g, S5_STATE, 1), a_im.reshape(g, S5_STATE, 1),
            a_re.reshape(g, 1, S5_STATE), a_im.reshape(g, 1, S5_STATE),
            log_dt.reshape(g, 1, 1),
            jnp.swapaxes(b_re, 1, 2), jnp.swapaxes(b_im, 1, 2),
            c_re, c_im,
            jnp.swapaxes(c_re, 1, 2), jnp.swapaxes(c_im, 1, 2),
            d.reshape(g, S5_GROUP_CH, 1))


_C_GQ, _C_GK, _C_GV, _C_GR = 0, 512, 1024, 1536
_C_MQ, _C_MK, _C_MV, _C_MO = 2048, 2560, 3072, 3584
_C_G0 = 4096
_C_LR = _C_G0 + 3 * D_MODEL
_C_IF = _C_LR + LANES
_W_COLS = _C_IF + LANES


def _mixer_kernel(x_ref, yst_ref, gpre_ref, gpost_ref, win_ref,
                  wgk_ref, bgk_ref, gnorm_ref, wgp_ref,
                  wglu_ref, bglu_ref, wsp_ref,
                  cw_ref, cb_ref, bif_ref, wmp_ref, wout_ref,
                  o_ref,
                  gq_sc, gk_sc, gv_sc, gla_sc, go_sc, gst_sc,
                  mqk_sc, mv_sc, mg_sc, mh_sc, mc_sc, mn_sc, mm_sc, carry_sc):
    tm = x_ref.shape[0]
    nchunk = tm // CHUNK
    hd = HEAD_DIM

    @pl.when(pl.program_id(0) == 0)
    def _init():
        gst_sc[...] = jnp.zeros_like(gst_sc)
        mc_sc[...] = jnp.zeros_like(mc_sc)
        mn_sc[...] = jnp.zeros_like(mn_sc)
        mm_sc[...] = jnp.zeros_like(mm_sc)
        carry_sc[...] = jnp.zeros_like(carry_sc)

    x = x_ref[...]
    u = _rms_norm(x, gpre_ref[...]).astype(BF16)

    def proj(c0, width):
        return _dot(u, win_ref[:, c0:c0 + width])

    li = lax.broadcasted_iota(jnp.int32, (CHUNK, CHUNK), 0)
    mi = lax.broadcasted_iota(jnp.int32, (CHUNK, CHUNK), 1)
    causal = li >= mi
    tri = causal.astype(BF16)

    gq_sc[...] = proj(_C_GQ, BRANCH)
    gk_sc[...] = proj(_C_GK, BRANCH)
    gv_sc[...] = proj(_C_GV, BRANCH)
    g_lr = proj(_C_LR, LANES).astype(BF16)
    gla_sc[...] = _log_sigmoid(_dot(g_lr, wgk_ref[...]) + bgk_ref[...]) / GLA_GATE_NORM

    def gla_chunk(c, carry):
        r0 = pl.multiple_of(c * CHUNK, CHUNK)
        rows = pl.ds(r0, CHUNK)
        b = _cumsum_rows(tri, gla_sc[rows, :])
        b_last = b[CHUNK - 1:CHUNK, :]
        q = gq_sc[rows, :] * (hd ** -0.5)
        k = gk_sc[rows, :]
        q_dec = (q * jnp.exp(b)).astype(BF16)
        k_inv = (k * jnp.exp(-b)).astype(BF16)
        k_end = (k * jnp.exp(b_last - b)).astype(BF16)
        v = gv_sc[rows, :].astype(BF16)
        decay = jnp.exp(b_last)
        for h in range(HEADS):
            sl = slice(h * hd, (h + 1) * hd)
            scores = jnp.where(causal, _dot_nt(q_dec[:, sl], k_inv[:, sl]), 0.0).astype(BF16)
            st = gst_sc[h]
            go_sc[rows, sl] = _dot_nt(q_dec[:, sl], st.astype(BF16)) + _dot(scores, v[:, sl])
            gst_sc[h] = st * decay[:, sl] + _dot_tn(v[:, sl], k_end[:, sl])
        return carry

    lax.fori_loop(0, nchunk, gla_chunk, 0)

    o = go_sc[...]
    g_r = _silu(proj(_C_GR, BRANCH))
    gnorm = gnorm_ref[...]
    o_heads = [_rms_norm(o[:, h * hd:(h + 1) * hd], gnorm) for h in range(HEADS)]
    o_gated = (jnp.concatenate(o_heads, axis=1) * g_r).astype(BF16)
    y_gla = _dot(o_gated, wgp_ref[...])

    qk_pre = jnp.concatenate([proj(_C_MQ, BRANCH), proj(_C_MK, BRANCH)], axis=1)
    mqk_sc[0:SUBLANES, :] = carry_sc[...]
    mqk_sc[SUBLANES:, :] = qk_pre
    carry_sc[...] = qk_pre[tm - SUBLANES:, :]
    cw = cw_ref[...]
    conv = cb_ref[...] + cw[ML_CONV - 1:ML_CONV, :] * qk_pre
    for j in range(1, ML_CONV):
        conv = conv + cw[ML_CONV - 1 - j:ML_CONV - j, :] * mqk_sc[SUBLANES - j:SUBLANES - j + tm, :]
    qk_act = _silu(conv)
    mqk_sc[0:tm, 0:BRANCH] = qk_act[:, 0:BRANCH]
    mqk_sc[0:tm, BRANCH:] = qk_act[:, BRANCH:] * (hd ** -0.5)
    mv_sc[...] = proj(_C_MV, BRANCH)
    pre_if = proj(_C_IF, LANES) + bif_ref[...]
    lane = lax.broadcasted_iota(jnp.int32, (tm, LANES), 1)
    mg_sc[...] = jnp.where(lane < HEADS, pre_if, _log_sigmoid(pre_if))

    def ml_chunk(c, carry):
        r0 = pl.multiple_of(c * CHUNK, CHUNK)
        rows = pl.ds(r0, CHUNK)
        g = mg_sc[rows, :]
        cs = _cumsum_rows(tri, g)
        lane_c = lax.broadcasted_iota(jnp.int32, (CHUNK, LANES), 1)
        z_t = jnp.where(lane_c < HEADS, g, cs).T
        q_all = mqk_sc[rows, 0:BRANCH]
        k_all = mqk_sc[rows, BRANCH:]
        v_all = mv_sc[rows, :].astype(BF16)
        for h in range(HEADS):
            sl = slice(h * hd, (h + 1) * hd)
            b_col = cs[:, HEADS + h:HEADS + h + 1]
            li_col = g[:, h:h + 1]
            b_row = z_t[HEADS + h:HEADS + h + 1, :]
            li_row = z_t[h:h + 1, :]
            b_last = b_col[CHUNK - 1:CHUNK, :]
            m_prev = mm_sc[h][0:1, 0:1]
            c_prev = mc_sc[h]
            n_prev = mn_sc[h][0:1, :]
            q = q_all[:, sl]
            k = k_all[:, sl]
            qb = q.astype(BF16)
            v = v_all[:, sl]
            log_w = jnp.where(causal, b_col - b_row + li_row, -jnp.inf)
            m_inter = b_col + m_prev
            m = jnp.maximum(m_inter, jnp.max(log_w, axis=-1, keepdims=True))
            s_inter = jnp.exp(m_inter - m)
            qk = _dot_nt(qb, k.astype(BF16)) * jnp.exp(log_w - m)
            num = s_inter * _dot(qb, c_prev.astype(BF16)) + _dot(qk.astype(BF16), v)
            den = (s_inter * jnp.sum(q * n_prev, axis=-1, keepdims=True)
                   + jnp.sum(qk, axis=-1, keepdims=True))
            mh_sc[rows, sl] = num / jnp.maximum(jnp.abs(den), jnp.exp(-m))
            g_end = b_last - b_col + li_col
            m_new = jnp.maximum(b_last + m_prev, jnp.max(g_end, axis=0, keepdims=True))
            kw = k * jnp.exp(g_end - m_new)
            sc = jnp.exp(b_last + m_prev - m_new)
            mc_sc[h] = sc * c_prev + _dot_tn(kw.astype(BF16), v)
            mn_sc[h] = jnp.broadcast_to(sc * n_prev + jnp.sum(kw, axis=0, keepdims=True),
                                        (SUBLANES, hd))
            mm_sc[h] = jnp.broadcast_to(m_new, (SUBLANES, LANES))
        return carry

    lax.fori_loop(0, nchunk, ml_chunk, 0)

    hid = (_sigmoid(proj(_C_MO, BRANCH)) * mh_sc[...]).astype(BF16)
    y_ml = _dot(hid, wmp_ref[...])

    ys = _gelu_tanh(yst_ref[...].T)
    glu = _sigmoid(_dot(ys.astype(BF16), wglu_ref[...]) + bglu_ref[...])
    y_s5 = _dot((ys * glu).astype(BF16), wsp_ref[...])

    merged = _sigmoid(proj(_C_G0, D_MODEL)) * y_gla
    merged = merged + _sigmoid(proj(_C_G0 + D_MODEL, D_MODEL)) * y_s5
    merged = merged + _sigmoid(proj(_C_G0 + 2 * D_MODEL, D_MODEL)) * y_ml
    mix = _dot(merged.astype(BF16), wout_ref[...])
    o_ref[...] = x + _rms_norm(mix, gpost_ref[...])


def _mixer(x, ys_t, w, tile):
    s = x.shape[0]
    weights = (w["gpre"], w["gpost"], w["win"], w["wgk"], w["bgk"], w["gnorm"], w["wgp"],
               w["wglu"], w["bglu"], w["wsp"], w["cw"], w["cb"], w["bif"], w["wmp"], w["wout"])
    return pl.pallas_call(
        _mixer_kernel,
        grid=(s // tile,),
        in_specs=[pl.BlockSpec((tile, D_MODEL), lambda i: (i, 0)),
                  pl.BlockSpec((BRANCH, tile), lambda i: (0, i))]
                 + [_full_spec(a.shape) for a in weights],
        out_specs=pl.BlockSpec((tile, D_MODEL), lambda i: (i, 0)),
        out_shape=jax.ShapeDtypeStruct((s, D_MODEL), F32),
        scratch_shapes=[pltpu.VMEM((tile, BRANCH), F32),
                        pltpu.VMEM((tile, BRANCH), F32),
                        pltpu.VMEM((tile, BRANCH), F32),
                        pltpu.VMEM((tile, BRANCH), F32),
                        pltpu.VMEM((tile, BRANCH), F32),
                        pltpu.VMEM((HEADS, HEAD_DIM, HEAD_DIM), F32),
                        pltpu.VMEM((tile + SUBLANES, 2 * BRANCH), F32),
                        pltpu.VMEM((tile, BRANCH), F32),
                        pltpu.VMEM((tile, LANES), F32),
                        pltpu.VMEM((tile, BRANCH), F32),
                        pltpu.VMEM((HEADS, HEAD_DIM, HEAD_DIM), F32),
                        pltpu.VMEM((HEADS, SUBLANES, HEAD_DIM), F32),
                        pltpu.VMEM((HEADS, SUBLANES, LANES), F32),
                        pltpu.VMEM((SUBLANES, 2 * BRANCH), F32)],
        compiler_params=pltpu.CompilerParams(dimension_semantics=("arbitrary",),
                                             vmem_limit_bytes=VMEM_LIMIT),
        name="mixer",
    )(x, ys_t, *weights)


def _ffn_kernel(x_ref, gpre_ref, gpost_ref, wup_ref, wgate_ref, cw_ref, cb_ref, wdown_ref,
                o_ref, u_sc, acc_sc, a_sc, carry_sc):
    i, f = pl.program_id(0), pl.program_id(1)
    tm = x_ref.shape[0]

    @pl.when(f == 0)
    def _norm():
        u_sc[...] = _rms_norm(x_ref[...], gpre_ref[...]).astype(BF16)

    @pl.when(i == 0)
    def _zero_carry():
        carry_sc[f] = jnp.zeros(carry_sc.shape[1:], F32)

    u = u_sc[...]
    a = _dot(u, wup_ref[...])
    gate = _dot(u, wgate_ref[...])
    a_sc[0:SUBLANES, :] = carry_sc[f]
    a_sc[SUBLANES:, :] = a
    carry_sc[f] = a[tm - SUBLANES:, :]
    cw = cw_ref[...]
    conv = cb_ref[...] + cw[FFN_CONV - 1:FFN_CONV, :] * a
    for j in range(1, FFN_CONV):
        conv = conv + cw[FFN_CONV - 1 - j:FFN_CONV - j, :] * a_sc[SUBLANES - j:SUBLANES - j + tm, :]
    h = (_gelu_tanh(conv) * gate).astype(BF16)
    part = _dot(h, wdown_ref[...])

    @pl.when(f == 0)
    def _first():
        acc_sc[...] = part

    @pl.when(f != 0)
    def _rest():
        acc_sc[...] += part

    @pl.when(f == pl.num_programs(1) - 1)
    def _finish():
        o_ref[...] = x_ref[...] + _rms_norm(acc_sc[...], gpost_ref[...])


def _ffn(x, w, tile, ftile):
    s = x.shape[0]
    nf = D_FF // ftile
    return pl.pallas_call(
        _ffn_kernel,
        grid=(s // tile, nf),
        in_specs=[pl.BlockSpec((tile, D_MODEL), lambda i, f: (i, 0)),
                  pl.BlockSpec((1, D_MODEL), lambda i, f: (0, 0)),
                  pl.BlockSpec((1, D_MODEL), lambda i, f: (0, 0)),
                  pl.BlockSpec((D_MODEL, ftile), lambda i, f: (0, f)),
                  pl.BlockSpec((D_MODEL, ftile), lambda i, f: (0, f)),
                  pl.BlockSpec((FFN_CONV, ftile), lambda i, f: (0, f)),
                  pl.BlockSpec((1, ftile), lambda i, f: (0, f)),
                  pl.BlockSpec((ftile, D_MODEL), lambda i, f: (f, 0))],
        out_specs=pl.BlockSpec((tile, D_MODEL), lambda i, f: (i, 0)),
        out_shape=jax.ShapeDtypeStruct((s, D_MODEL), F32),
        scratch_shapes=[pltpu.VMEM((tile, D_MODEL), BF16),
                        pltpu.VMEM((tile, D_MODEL), F32),
                        pltpu.VMEM((tile + SUBLANES, ftile), F32),
                        pltpu.VMEM((nf, SUBLANES, ftile), F32)],
        compiler_params=pltpu.CompilerParams(dimension_semantics=("arbitrary", "arbitrary"),
                                             vmem_limit_bytes=VMEM_LIMIT),
        name="conv_ffn",
    )(x, w["gpre"], w["gpost"], w["wup"], w["wgate"], w["cw"], w["cb"], w["wdown"])


def _pad_cols(a, width):
    return jnp.pad(a, ((0, 0), (0, width - a.shape[1])))


def _pack_layer(l, p):
    w_in = p["w_in"][l]
    sp = (0, 512, 1024, 1536, 1552, 2064, 2576, 3088, 3600, 4112, 4116, 4120, 4632, 7704)
    (g_q, g_k, g_v, g_lr, g_r, s_u, m_q, m_k, m_v, m_i, m_f, m_o, gates) = [
        w_in[:, sp[j]:sp[j + 1]] for j in range(13)]
    win = jnp.concatenate(
        [g_q, g_k, g_v, g_r, m_q, m_k, m_v, m_o, gates,
         _pad_cols(g_lr, LANES), _pad_cols(jnp.concatenate([m_i, m_f], axis=1), LANES)],
        axis=1).astype(BF16)
    row = lambda a: a.reshape(1, -1)
    mixer = dict(
        gpre=row(p["norm_mix_pre"][l]), gpost=row(p["norm_mix_post"][l]), win=win,
        wgk=jnp.pad(p["gla_w_gk"][l], ((0, LANES - GLA_RANK), (0, 0))).astype(BF16),
        bgk=row(p["gla_b_gk"][l]), gnorm=row(p["gla_norm"][l]),
        wgp=p["gla_w_proj"][l].astype(BF16),
        wglu=p["s5_w_glu"][l].astype(BF16), bglu=row(p["s5_b_glu"][l]),
        wsp=p["s5_w_proj"][l].astype(BF16),
        cw=p["ml_conv_w"][l], cb=row(p["ml_conv_b"][l]),
        bif=_pad_cols(row(jnp.concatenate([p["ml_b_i"][l], p["ml_b_f"][l]])), LANES),
        wmp=p["ml_w_proj"][l].astype(BF16), wout=p["w_out"][l].astype(BF16))
    s5_w = s_u.T.astype(BF16)
    s5_prm = _s5_params(p["s5_a_re"][l], p["s5_a_im"][l], p["s5_log_dt"][l], p["s5_b_re"][l],
                        p["s5_b_im"][l], p["s5_c_re"][l], p["s5_c_im"][l], p["s5_d"][l])
    ffn = dict(gpre=row(p["norm_ffn_pre"][l]), gpost=row(p["norm_ffn_post"][l]),
               wup=p["ffn_w_up"][l].astype(BF16), wgate=p["ffn_w_gate"][l].astype(BF16),
               cw=p["ffn_conv_w"][l], cb=row(p["ffn_conv_b"][l]),
               wdown=p["ffn_w_down"][l].astype(BF16))
    return mixer, s5_w, s5_prm, ffn


def _tiles(s):
    return min(1024, s), min(512, s), min(1024, s), D_FF // 2


def kernel(x, norm_mix_pre, norm_mix_post, norm_ffn_pre, norm_ffn_post, w_in, gla_w_gk, gla_b_gk, gla_norm, gla_w_proj, s5_a_re, s5_a_im, s5_log_dt, s5_b_re, s5_b_im, s5_c_re, s5_c_im, s5_d, s5_w_glu, s5_b_glu, s5_w_proj, ml_conv_w, ml_conv_b, ml_b_i, ml_b_f, ml_w_proj, w_out, ffn_w_up, ffn_w_gate, ffn_conv_w, ffn_conv_b, ffn_w_down):
    p = dict(norm_mix_pre=norm_mix_pre, norm_mix_post=norm_mix_post, norm_ffn_pre=norm_ffn_pre,
             norm_ffn_post=norm_ffn_post, w_in=w_in, gla_w_gk=gla_w_gk, gla_b_gk=gla_b_gk,
             gla_norm=gla_norm, gla_w_proj=gla_w_proj, s5_a_re=s5_a_re, s5_a_im=s5_a_im,
             s5_log_dt=s5_log_dt, s5_b_re=s5_b_re, s5_b_im=s5_b_im, s5_c_re=s5_c_re,
             s5_c_im=s5_c_im, s5_d=s5_d, s5_w_glu=s5_w_glu, s5_b_glu=s5_b_glu, s5_w_proj=s5_w_proj,
             ml_conv_w=ml_conv_w, ml_conv_b=ml_conv_b, ml_b_i=ml_b_i, ml_b_f=ml_b_f,
             ml_w_proj=ml_w_proj, w_out=w_out, ffn_w_up=ffn_w_up, ffn_w_gate=ffn_w_gate,
             ffn_conv_w=ffn_conv_w, ffn_conv_b=ffn_conv_b, ffn_w_down=ffn_w_down)
    bsz, s, _ = x.shape
    assert bsz == 1 and s % (S5_BLOCK * SUBLANES) == 0
    t_s5, t_mix, t_ffn, t_ff = _tiles(s)
    h = x[0]
    for l in range(w_in.shape[0]):
        mixer_w, s5_w, s5_prm, ffn_w = _pack_layer(l, p)
        u_t = _s5_in_proj(h, mixer_w["gpre"], s5_w, t_s5)
        y_t = _s5_core(u_t, s5_prm)
        h = _mixer(h, y_t, mixer_w, t_mix)
        h = _ffn(h, ffn_w, t_ffn, t_ff)
    return h[None]
```

```python
import functools
import math

import jax
import jax.numpy as jnp
from jax import lax
from jax.experimental import pallas as pl
from jax.experimental.pallas import tpu as pltpu

F32 = jnp.float32
BF16 = jnp.bfloat16

D_MODEL = 1024
CHUNK = 64
NORM_EPS = 1e-6
BRANCH = 512
HEADS = 4
HEAD_DIM = BRANCH // HEADS
GLA_RANK = 16
GLA_GATE_NORM = 16.0
S5_GROUP_CH = 16
S5_GROUPS = BRANCH // S5_GROUP_CH
S5_STATE = 64
S5_BLOCK = 128
ML_CONV = 4
D_FF = 2816
FFN_CONV = 3
LANES = 128
SUBLANES = 8

VMEM_LIMIT = 56 * 1024 * 1024


def _rms_norm(x, gain):
    return x * lax.rsqrt(jnp.mean(x * x, axis=-1, keepdims=True) + NORM_EPS) * gain


def _dot(a, b):
    return jnp.dot(a, b, preferred_element_type=F32)


def _dot_nt(a, b):
    return lax.dot_general(a, b, (((1,), (1,)), ((), ())), preferred_element_type=F32)


def _log_sigmoid(x):
    return jnp.minimum(x, 0.0) - jnp.log1p(jnp.exp(-jnp.abs(x)))


def _sigmoid(x):
    return 1.0 / (1.0 + jnp.exp(-x))


def _silu(x):
    return x * _sigmoid(x)


def _gelu_tanh(x):
    c = math.sqrt(2.0 / math.pi)
    return x * (0.5 * (1.0 + jnp.tanh(c * (x + 0.044715 * (x * x * x)))))


def _full_spec(shape):
    zeros = (0,) * len(shape)
    return pl.BlockSpec(shape, lambda *_: zeros, pipeline_mode=pl.Buffered(1))


def _s5_in_kernel(x_ref, gain_ref, w_ref, o_ref):
    u = _rms_norm(x_ref[...], gain_ref[...]).astype(BF16)
    o_ref[...] = _dot_nt(w_ref[...], u)


def _s5_in_proj(x, gain, w_t, tile):
    s = x.shape[0]
    return pl.pallas_call(
        _s5_in_kernel,
        grid=(s // tile,),
        in_specs=[pl.BlockSpec((tile, D_MODEL), lambda i: (i, 0)),
                  _full_spec((1, D_MODEL)),
                  _full_spec((BRANCH, D_MODEL))],
        out_specs=pl.BlockSpec((BRANCH, tile), lambda i: (0, i)),
        out_shape=jax.ShapeDtypeStruct((BRANCH, s), F32),
        compiler_params=pltpu.CompilerParams(dimension_semantics=("arbitrary",),
                                             vmem_limit_bytes=VMEM_LIMIT),
        name="s5_in_proj",
    )(x, gain, w_t)


def _cmul(xr, xi, yr, yi):
    return xr * yr - xi * yi, xr * yi + xi * yr


def _s5_kernel(u_ref, are_c_ref, aim_c_ref, are_r_ref, aim_r_ref, ldt_ref,
               bt_re_ref, bt_im_ref, c_re_ref, c_im_ref, ct_re_ref, ct_im_ref, d_ref,
               o_ref, k_sc, t_sc, win_sc, m_sc):
    nch, nblk, blk = u_ref.shape
    half = S5_STATE
    dt = jnp.exp(ldt_ref[...])

    def discretize(are, aim):
        rho, th = dt * are, dt * aim
        mag = jnp.exp(rho)
        abr, abi = mag * jnp.cos(th), mag * jnp.sin(th)
        den = are * are + aim * aim
        zr = abr - 1.0
        fr = (zr * are + abi * aim) / den
        fi = (abi * are - zr * aim) / den
        return rho, th, abr, abi, fr, fi

    rho_c, th_c, abr_c, abi_c, _, _ = discretize(are_c_ref[...], aim_c_ref[...])
    rho_r, th_r, _, _, fr_r, fi_r = discretize(are_r_ref[...], aim_r_ref[...])

    bbr, bbi = _cmul(fr_r, fi_r, bt_re_ref[...], bt_im_ref[...])

    lag_l = lax.broadcasted_iota(jnp.int32, (half, blk), 1).astype(F32)
    pmag = jnp.exp(rho_c * lag_l)
    p_re, p_im = pmag * jnp.cos(th_c * lag_l), pmag * jnp.sin(th_c * lag_l)

    c_re, c_im = c_re_ref[...], c_im_ref[...]
    cb_rows = []
    for hi in range(nch):
        cbr, cbi = _cmul(c_re, c_im, bbr[hi:hi + 1, :], bbi[hi:hi + 1, :])
        cb_rows.append(jnp.concatenate([cbr, cbi], axis=1))
    cb = jnp.concatenate(cb_rows, axis=0)
    pstack = jnp.concatenate([p_re, -p_im], axis=0)
    kern = jnp.dot(cb, pstack, preferred_element_type=F32, precision=lax.Precision.HIGHEST)
    rr = lax.broadcasted_iota(jnp.int32, (nch * nch, blk), 0)
    ll = lax.broadcasted_iota(jnp.int32, (nch * nch, blk), 1)
    d_col = jnp.concatenate([d_ref[...]] * nch, axis=0)
    diag = ((rr >> 4) == (rr & (nch - 1))) & (ll == 0)
    k_sc[...] = kern + jnp.where(diag, d_col, 0.0)

    srow = lax.broadcasted_iota(jnp.int32, (blk, blk), 0)
    lcol = lax.broadcasted_iota(jnp.int32, (blk, blk), 1)
    lower = lcol >= srow

    def build_t(hi, carry):
        r0 = pl.multiple_of(hi * blk, blk)
        for ho in range(nch):
            krow = k_sc[pl.ds(hi * nch + ho, 1), :]
            tile = pltpu.roll(jnp.broadcast_to(krow, (blk, blk)), 0, 1, stride=1, stride_axis=0)
            t_sc[pl.ds(r0, blk), ho * blk:(ho + 1) * blk] = jnp.where(lower, tile, 0.0).astype(BF16)
        return carry

    lax.fori_loop(0, nch, build_t, 0)

    lag_s = (blk - 1 - lax.broadcasted_iota(jnp.int32, (blk, half), 0)).astype(F32)
    qmag = jnp.exp(rho_r * lag_s)
    q_re, q_im = qmag * jnp.cos(th_r * lag_s), qmag * jnp.sin(th_r * lag_s)
    for hi in range(nch):
        wr, wi = _cmul(q_re, q_im, bbr[hi:hi + 1, :], bbi[hi:hi + 1, :])
        win_sc[hi * blk:(hi + 1) * blk, :] = jnp.concatenate([wr, wi], axis=1).astype(BF16)

    p1_re, p1_im = _cmul(p_re, p_im, abr_c, abi_c)
    ct_re, ct_im = ct_re_ref[...], ct_im_ref[...]
    for ho in range(nch):
        mr, mi = _cmul(p1_re, p1_im, ct_re[:, ho:ho + 1], ct_im[:, ho:ho + 1])
        m_sc[:, ho * blk:(ho + 1) * blk] = jnp.concatenate([mr, -mi], axis=0).astype(BF16)

    ucat = jnp.concatenate([u_ref[h] for h in range(nch)], axis=1).astype(BF16)

    x = _dot(ucat, win_sc[...])
    blk_f = float(blk)
    wmag = jnp.exp(rho_r * blk_f)
    w_re, w_im = wmag * jnp.cos(th_r * blk_f), wmag * jnp.sin(th_r * blk_f)
    row = lax.broadcasted_iota(jnp.int32, (nblk, 2 * half), 0)

    def shift_rows(a, k):
        return jnp.where(row >= k, pltpu.roll(a, k, 0), 0.0)

    def state_mul(a, wr, wi):
        return (a * jnp.concatenate([wr, wr], axis=1)
                + pltpu.roll(a, half, 1) * jnp.concatenate([-wi, wi], axis=1))

    step = 1
    while step < nblk:
        x = x + state_mul(shift_rows(x, step), w_re, w_im)
        w_re, w_im = _cmul(w_re, w_im, w_re, w_im)
        step *= 2
    x_start = shift_rows(x, 1).astype(BF16)

    y = _dot(ucat, t_sc[...]) + _dot(x_start, m_sc[...])
    for ho in range(nch):
        o_ref[ho] = y[:, ho * blk:(ho + 1) * blk]


def _s5_core(u_t, prm):
    s = u_t.shape[1]
    nblk = s // S5_BLOCK
    u4 = u_t.reshape(S5_GROUPS, S5_GROUP_CH, nblk, S5_BLOCK)

    def gspec(*dims):
        zeros = (0,) * len(dims)
        return pl.BlockSpec((None,) + dims, lambda g: (g,) + zeros)

    width = S5_GROUP_CH * S5_BLOCK
    out = pl.pallas_call(
        _s5_kernel,
        grid=(S5_GROUPS,),
        in_specs=[gspec(S5_GROUP_CH, nblk, S5_BLOCK),
                  gspec(S5_STATE, 1), gspec(S5_STATE, 1), gspec(1, S5_STATE), gspec(1, S5_STATE),
                  gspec(1, 1),
                  gspec(S5_GROUP_CH, S5_STATE), gspec(S5_GROUP_CH, S5_STATE),
                  gspec(S5_GROUP_CH, S5_STATE), gspec(S5_GROUP_CH, S5_STATE),
                  gspec(S5_STATE, S5_GROUP_CH), gspec(S5_STATE, S5_GROUP_CH),
                  gspec(S5_GROUP_CH, 1)],
        out_specs=gspec(S5_GROUP_CH, nblk, S5_BLOCK),
        out_shape=jax.ShapeDtypeStruct(u4.shape, F32),
        scratch_shapes=[pltpu.VMEM((S5_GROUP_CH * S5_GROUP_CH, S5_BLOCK), F32),
                        pltpu.VMEM((width, width), BF16),
                        pltpu.VMEM((width, 2 * S5_STATE), BF16),
                        pltpu.VMEM((2 * S5_STATE, width), BF16)],
        compiler_params=pltpu.CompilerParams(dimension_semantics=("arbitrary",),
                                             vmem_limit_bytes=VMEM_LIMIT),
        name="s5_core",
    )(u4, *prm)
    return out.reshape(BRANCH, s)


def _s5_params(a_re, a_im, log_dt, b_re, b_im, c_re, c_im, d):
    g = S5_GROUPS
    return (a_re.reshape(g, S5_STATE, 1), a_im.reshape(g, S5_STATE, 1),
            a_re.reshape(g, 1, S5_STATE), a_im.reshape(g, 1, S5_STATE),
            log_dt.reshape(g, 1, 1),
            jnp.swapaxes(b_re, 1, 2), jnp.swapaxes(b_im, 1, 2),
            c_re, c_im,
            jnp.swapaxes(c_re, 1, 2), jnp.swapaxes(c_im, 1, 2),
            d.reshape(g, S5_GROUP_CH, 1))


_C_GQ, _C_GK, _C_GV, _C_GR = 0, 512, 1024, 1536
_C_MQ, _C_MK, _C_MV, _C_MO = 2048, 2560, 3072, 3584
_C_G0 = 4096
_C_LR = _C_G0 + 3 * D_MODEL
_C_IF = _C_LR + LANES
_W_COLS = _C_IF + LANES

GROUP = 4 * CHUNK
_LOG2_CHUNK = CHUNK.bit_length() - 1
_LOG2_HEAD_DIM = HEAD_DIM.bit_length() - 1


def _group_masks():
    r = lax.broadcasted_iota(jnp.int32, (GROUP, GROUP), 0)
    c = lax.broadcasted_iota(jnp.int32, (GROUP, GROUP), 1)
    causal = ((r >> _LOG2_CHUNK) == (c >> _LOG2_CHUNK)) & (r >= c)
    rr = lax.broadcasted_iota(jnp.int32, (4 * HEAD_DIM, GROUP), 0)
    cc = lax.broadcasted_iota(jnp.int32, (4 * HEAD_DIM, GROUP), 1)
    stacked = (rr >> _LOG2_HEAD_DIM) == (cc >> _LOG2_CHUNK)
    return causal, stacked


def _cumsum_chunks(tri, a):
    hi = a.astype(BF16)
    mid = (a - hi.astype(F32)).astype(BF16)
    return _dot(tri, hi) + _dot(tri, mid)


def _chunk_rows(c):
    return slice(c * CHUNK, (c + 1) * CHUNK)


def _per_chunk_sums(x_t, stacked, rhs):
    lhs = jnp.where(stacked, jnp.concatenate([x_t] * 4, axis=0), 0.0).astype(BF16)
    return _dot(lhs, rhs)


def _gla_group(q, k, v, la, states, causal, tri, stacked):
    hd = HEAD_DIM
    b = _cumsum_chunks(tri, la)
    last = [b[c * CHUNK + CHUNK - 1:(c + 1) * CHUNK, :] for c in range(4)]
    b_last = jnp.concatenate([jnp.broadcast_to(r, (CHUNK, BRANCH)) for r in last], axis=0)
    q_dec = (q * (hd ** -0.5) * jnp.exp(b)).astype(BF16)
    k_inv = (k * jnp.exp(-b)).astype(BF16)
    k_end = (k * jnp.exp(b_last - b)).astype(BF16)
    vb = v.astype(BF16)
    decay = [jnp.exp(r) for r in last]
    outs, new_states = [], []
    for h in range(HEADS):
        sl = slice(h * hd, (h + 1) * hd)
        scores = jnp.where(causal, _dot_nt(q_dec[:, sl], k_inv[:, sl]), 0.0).astype(BF16)
        intra = _dot(scores, vb[:, sl])
        kv = _per_chunk_sums(v[:, sl].T, stacked, k_end[:, sl])
        st = states[h]
        inter = []
        for c in range(4):
            inter.append(_dot_nt(q_dec[_chunk_rows(c), sl], st.astype(BF16)))
            st = st * decay[c][:, sl] + kv[c * hd:(c + 1) * hd, :]
        new_states.append(st)
        outs.append(intra + jnp.concatenate(inter, axis=0))
    return jnp.concatenate(outs, axis=1), new_states


def _mlstm_group(q, k, v, gates, states, m_run, causal, tri, stacked):
    hd = HEAD_DIM
    cs = _cumsum_chunks(tri, gates)
    log_i = pltpu.roll(gates, HEADS, 1)
    w_rows, m_prev_rows, scale = [], [], []
    for c in range(4):
        rows = _chunk_rows(c)
        b_last = cs[c * CHUNK + CHUNK - 1:(c + 1) * CHUNK, :]
        g_end = b_last - cs[rows, :] + log_i[rows, :]
        m_new = jnp.maximum(b_last + m_run, jnp.max(g_end, axis=0, keepdims=True))
        w_rows.append(jnp.exp(g_end - m_new))
        scale.append(jnp.exp(b_last + m_run - m_new))
        m_prev_rows.append(jnp.broadcast_to(m_run, (CHUNK, LANES)))
        m_run = m_new
    w_all = jnp.concatenate(w_rows, axis=0)
    m_inter = cs + jnp.concatenate(m_prev_rows, axis=0)
    lane = lax.broadcasted_iota(jnp.int32, (GROUP, LANES), 1)
    z_t = jnp.where(lane < HEADS, gates, cs).T
    ones = jnp.ones((GROUP, hd), BF16)
    outs, new_states = [], []
    for h in range(HEADS):
        sl = slice(h * hd, (h + 1) * hd)
        col = HEADS + h
        log_w = jnp.where(causal, cs[:, col:col + 1] - z_t[col:col + 1, :] + z_t[h:h + 1, :], -jnp.inf)
        m_int = m_inter[:, col:col + 1]
        m = jnp.maximum(m_int, jnp.max(log_w, axis=-1, keepdims=True))
        s_inter = jnp.exp(m_int - m)
        qb = q[:, sl].astype(BF16)
        kh = k[:, sl]
        v_aug = jnp.concatenate([v[:, sl].astype(BF16), ones], axis=1)
        qk = (_dot_nt(qb, kh.astype(BF16)) * jnp.exp(log_w - m)).astype(BF16)
        intra = _dot(qk, v_aug)
        kw_t = (kh * w_all[:, col:col + 1]).T
        kv = _per_chunk_sums(kw_t, stacked, v_aug)
        st = states[h]
        inter = []
        for c in range(4):
            inter.append(_dot(qb[_chunk_rows(c), :], st.astype(BF16)))
            st = scale[c][:, col:col + 1] * st + kv[c * hd:(c + 1) * hd, :]
        new_states.append(st)
        tot = s_inter * jnp.concatenate(inter, axis=0) + intra
        outs.append(tot[:, :hd] / jnp.maximum(jnp.abs(tot[:, hd:]), jnp.exp(-m)))
    return jnp.concatenate(outs, axis=1), new_states, m_run


def _mixer_kernel(x_ref, yst_ref, gpre_ref, gpost_ref, win_ref,
                  wgk_ref, bgk_ref, gnorm_ref, wgp_ref,
                  wglu_ref, bglu_ref, wsp_ref,
                  cw_ref, cb_ref, bif_ref, wmp_ref, wout_ref,
                  o_ref,
                  gst_sc, mqk_sc, mc_sc, mm_sc, carry_sc):
    tm = x_ref.shape[0]
    ngroup = tm // GROUP
    hd = HEAD_DIM

    @pl.when(pl.program_id(0) == 0)
    def _init():
        gst_sc[...] = jnp.zeros_like(gst_sc)
        mc_sc[...] = jnp.zeros_like(mc_sc)
        mm_sc[...] = jnp.zeros_like(mm_sc)
        carry_sc[...] = jnp.zeros_like(carry_sc)

    x = x_ref[...]
    u = _rms_norm(x, gpre_ref[...]).astype(BF16)

    def proj(c0, width):
        return _dot(u, win_ref[:, c0:c0 + width])

    causal, stacked = _group_masks()
    tri = causal.astype(BF16)

    gq, gk, gv = proj(_C_GQ, BRANCH), proj(_C_GK, BRANCH), proj(_C_GV, BRANCH)
    g_lr = proj(_C_LR, LANES).astype(BF16)
    log_a = _log_sigmoid(_dot(g_lr, wgk_ref[...]) + bgk_ref[...]) / GLA_GATE_NORM
    states = [gst_sc[h] for h in range(HEADS)]
    o_groups = []
    for g in range(ngroup):
        r = slice(g * GROUP, (g + 1) * GROUP)
        o_g, states = _gla_group(gq[r], gk[r], gv[r], log_a[r], states, causal, tri, stacked)
        o_groups.append(o_g)
    for h in range(HEADS):
        gst_sc[h] = states[h]
    o = jnp.concatenate(o_groups, axis=0)
    g_r = _silu(proj(_C_GR, BRANCH))
    gnorm = gnorm_ref[...]
    o_heads = [_rms_norm(o[:, h * hd:(h + 1) * hd], gnorm) for h in range(HEADS)]
    o_gated = (jnp.concatenate(o_heads, axis=1) * g_r).astype(BF16)
    y_gla = _dot(o_gated, wgp_ref[...])

    qk_pre = jnp.concatenate([proj(_C_MQ, BRANCH), proj(_C_MK, BRANCH)], axis=1)
    mqk_sc[0:SUBLANES, :] = carry_sc[...]
    mqk_sc[SUBLANES:, :] = qk_pre
    carry_sc[...] = qk_pre[tm - SUBLANES:, :]
    cw = cw_ref[...]
    conv = cb_ref[...] + cw[ML_CONV - 1:ML_CONV, :] * qk_pre
    for j in range(1, ML_CONV):
        conv = conv + cw[ML_CONV - 1 - j:ML_CONV - j, :] * mqk_sc[SUBLANES - j:SUBLANES - j + tm, :]
    qk_act = _silu(conv)
    mq = qk_act[:, 0:BRANCH]
    mk = qk_act[:, BRANCH:] * (hd ** -0.5)
    mv = proj(_C_MV, BRANCH)
    pre_if = proj(_C_IF, LANES) + bif_ref[...]
    lane = lax.broadcasted_iota(jnp.int32, (tm, LANES), 1)
    gates = jnp.where(lane < HEADS, pre_if, _log_sigmoid(pre_if))
    states = [mc_sc[h] for h in range(HEADS)]
    m_run = mm_sc[0:1, :]
    h_groups = []
    for g in range(ngroup):
        r = slice(g * GROUP, (g + 1) * GROUP)
        h_g, states, m_run = _mlstm_group(mq[r], mk[r], mv[r], gates[r], states, m_run,
                                          causal, tri, stacked)
        h_groups.append(h_g)
    for h in range(HEADS):
        mc_sc[h] = states[h]
    mm_sc[...] = jnp.broadcast_to(m_run, mm_sc.shape)
    hid = (_sigmoid(proj(_C_MO, BRANCH)) * jnp.concatenate(h_groups, axis=0)).astype(BF16)
    y_ml = _dot(hid, wmp_ref[...])

    ys = _gelu_tanh(yst_ref[...].T)
    glu = _sigmoid(_dot(ys.astype(BF16), wglu_ref[...]) + bglu_ref[...])
    y_s5 = _dot((ys * glu).astype(BF16), wsp_ref[...])

    merged = _sigmoid(proj(_C_G0, D_MODEL)) * y_gla
    merged = merged + _sigmoid(proj(_C_G0 + D_MODEL, D_MODEL)) * y_s5
    merged = merged + _sigmoid(proj(_C_G0 + 2 * D_MODEL, D_MODEL)) * y_ml
    mix = _dot(merged.astype(BF16), wout_ref[...])
    o_ref[...] = x + _rms_norm(mix, gpost_ref[...])


def _mixer(x, ys_t, w, tile):
    s = x.shape[0]
    weights = (w["gpre"], w["gpost"], w["win"], w["wgk"], w["bgk"], w["gnorm"], w["wgp"],
               w["wglu"], w["bglu"], w["wsp"], w["cw"], w["cb"], w["bif"], w["wmp"], w["wout"])
    return pl.pallas_call(
        _mixer_kernel,
        grid=(s // tile,),
        in_specs=[pl.BlockSpec((tile, D_MODEL), lambda i: (i, 0)),
                  pl.BlockSpec((BRANCH, tile), lambda i: (0, i))]
                 + [_full_spec(a.shape) for a in weights],
        out_specs=pl.BlockSpec((tile, D_MODEL), lambda i: (i, 0)),
        out_shape=jax.ShapeDtypeStruct((s, D_MODEL), F32),
        scratch_shapes=[pltpu.VMEM((HEADS, HEAD_DIM, HEAD_DIM), F32),
                        pltpu.VMEM((tile + SUBLANES, 2 * BRANCH), F32),
                        pltpu.VMEM((HEADS, HEAD_DIM, 2 * HEAD_DIM), F32),
                        pltpu.VMEM((SUBLANES, LANES), F32),
                        pltpu.VMEM((SUBLANES, 2 * BRANCH), F32)],
        compiler_params=pltpu.CompilerParams(dimension_semantics=("arbitrary",),
                                             vmem_limit_bytes=VMEM_LIMIT),
        name="mixer",
    )(x, ys_t, *weights)


def _ffn_kernel(x_ref, gpre_ref, gpost_ref, wup_ref, wgate_ref, cw_ref, cb_ref, wdown_ref,
                o_ref, u_sc, acc_sc, a_sc, carry_sc):
    i, f = pl.program_id(0), pl.program_id(1)
    tm = x_ref.shape[0]

    @pl.when(f == 0)
    def _norm():
        u_sc[...] = _rms_norm(x_ref[...], gpre_ref[...]).astype(BF16)

    @pl.when(i == 0)
    def _zero_carry():
        carry_sc[f] = jnp.zeros(carry_sc.shape[1:], F32)

    u = u_sc[...]
    a = _dot(u, wup_ref[...])
    gate = _dot(u, wgate_ref[...])
    a_sc[0:SUBLANES, :] = carry_sc[f]
    a_sc[SUBLANES:, :] = a
    carry_sc[f] = a[tm - SUBLANES:, :]
    cw = cw_ref[...]
    conv = cb_ref[...] + cw[FFN_CONV - 1:FFN_CONV, :] * a
    for j in range(1, FFN_CONV):
        conv = conv + cw[FFN_CONV - 1 - j:FFN_CONV - j, :] * a_sc[SUBLANES - j:SUBLANES - j + tm, :]
    h = (_gelu_tanh(conv) * gate).astype(BF16)
    part = _dot(h, wdown_ref[...])

    @pl.when(f == 0)
    def _first():
        acc_sc[...] = part

    @pl.when(f != 0)
    def _rest():
        acc_sc[...] += part

    @pl.when(f == pl.num_programs(1) - 1)
    def _finish():
        o_ref[...] = x_ref[...] + _rms_norm(acc_sc[...], gpost_ref[...])


def _ffn(x, w, tile, ftile):
    s = x.shape[0]
    nf = D_FF // ftile
    return pl.pallas_call(
        _ffn_kernel,
        grid=(s // tile, nf),
        in_specs=[pl.BlockSpec((tile, D_MODEL), lambda i, f: (i, 0)),
                  pl.BlockSpec((1, D_MODEL), lambda i, f: (0, 0)),
                  pl.BlockSpec((1, D_MODEL), lambda i, f: (0, 0)),
                  pl.BlockSpec((D_MODEL, ftile), lambda i, f: (0, f)),
                  pl.BlockSpec((D_MODEL, ftile), lambda i, f: (0, f)),
                  pl.BlockSpec((FFN_CONV, ftile), lambda i, f: (0, f)),
                  pl.BlockSpec((1, ftile), lambda i, f: (0, f)),
                  pl.BlockSpec((ftile, D_MODEL), lambda i, f: (f, 0))],
        out_specs=pl.BlockSpec((tile, D_MODEL), lambda i, f: (i, 0)),
        out_shape=jax.ShapeDtypeStruct((s, D_MODEL), F32),
        scratch_shapes=[pltpu.VMEM((tile, D_MODEL), BF16),
                        pltpu.VMEM((tile, D_MODEL), F32),
                        pltpu.VMEM((tile + SUBLANES, ftile), F32),
                        pltpu.VMEM((nf, SUBLANES, ftile), F32)],
        compiler_params=pltpu.CompilerParams(dimension_semantics=("arbitrary", "arbitrary"),
                                             vmem_limit_bytes=VMEM_LIMIT),
        name="conv_ffn",
    )(x, w["gpre"], w["gpost"], w["wup"], w["wgate"], w["cw"], w["cb"], w["wdown"])


def _pad_cols(a, width):
    return jnp.pad(a, ((0, 0), (0, width - a.shape[1])))


def _pack_layer(l, p):
    w_in = p["w_in"][l]
    sp = (0, 512, 1024, 1536, 1552, 2064, 2576, 3088, 3600, 4112, 4116, 4120, 4632, 7704)
    (g_q, g_k, g_v, g_lr, g_r, s_u, m_q, m_k, m_v, m_i, m_f, m_o, gates) = [
        w_in[:, sp[j]:sp[j + 1]] for j in range(13)]
    win = jnp.concatenate(
        [g_q, g_k, g_v, g_r, m_q, m_k, m_v, m_o, gates,
         _pad_cols(g_lr, LANES), _pad_cols(jnp.concatenate([m_i, m_f], axis=1), LANES)],
        axis=1).astype(BF16)
    row = lambda a: a.reshape(1, -1)
    mixer = dict(
        gpre=row(p["norm_mix_pre"][l]), gpost=row(p["norm_mix_post"][l]), win=win,
        wgk=jnp.pad(p["gla_w_gk"][l], ((0, LANES - GLA_RANK), (0, 0))).astype(BF16),
        bgk=row(p["gla_b_gk"][l]), gnorm=row(p["gla_norm"][l]),
        wgp=p["gla_w_proj"][l].astype(BF16),
        wglu=p["s5_w_glu"][l].astype(BF16), bglu=row(p["s5_b_glu"][l]),
        wsp=p["s5_w_proj"][l].astype(BF16),
        cw=p["ml_conv_w"][l], cb=row(p["ml_conv_b"][l]),
        bif=_pad_cols(row(jnp.concatenate([p["ml_b_i"][l], p["ml_b_f"][l]])), LANES),
        wmp=p["ml_w_proj"][l].astype(BF16), wout=p["w_out"][l].astype(BF16))
    s5_w = s_u.T.astype(BF16)
    s5_prm = _s5_params(p["s5_a_re"][l], p["s5_a_im"][l], p["s5_log_dt"][l], p["s5_b_re"][l],
                        p["s5_b_im"][l], p["s5_c_re"][l], p["s5_c_im"][l], p["s5_d"][l])
    ffn = dict(gpre=row(p["norm_ffn_pre"][l]), gpost=row(p["norm_ffn_post"][l]),
               wup=p["ffn_w_up"][l].astype(BF16), wgate=p["ffn_w_gate"][l].astype(BF16),
               cw=p["ffn_conv_w"][l], cb=row(p["ffn_conv_b"][l]),
               wdown=p["ffn_w_down"][l].astype(BF16))
    return mixer, s5_w, s5_prm, ffn


def _tiles(s):
    return min(1024, s), min(512, s), min(1024, s), D_FF // 2


def kernel(x, norm_mix_pre, norm_mix_post, norm_ffn_pre, norm_ffn_post, w_in, gla_w_gk, gla_b_gk, gla_norm, gla_w_proj, s5_a_re, s5_a_im, s5_log_dt, s5_b_re, s5_b_im, s5_c_re, s5_c_im, s5_d, s5_w_glu, s5_b_glu, s5_w_proj, ml_conv_w, ml_conv_b, ml_b_i, ml_b_f, ml_w_proj, w_out, ffn_w_up, ffn_w_gate, ffn_conv_w, ffn_conv_b, ffn_w_down):
    p = dict(norm_mix_pre=norm_mix_pre, norm_mix_post=norm_mix_post, norm_ffn_pre=norm_ffn_pre,
             norm_ffn_post=norm_ffn_post, w_in=w_in, gla_w_gk=gla_w_gk, gla_b_gk=gla_b_gk,
             gla_norm=gla_norm, gla_w_proj=gla_w_proj, s5_a_re=s5_a_re, s5_a_im=s5_a_im,
             s5_log_dt=s5_log_dt, s5_b_re=s5_b_re, s5_b_im=s5_b_im, s5_c_re=s5_c_re,
             s5_c_im=s5_c_im, s5_d=s5_d, s5_w_glu=s5_w_glu, s5_b_glu=s5_b_glu, s5_w_proj=s5_w_proj,
             ml_conv_w=ml_conv_w, ml_conv_b=ml_conv_b, ml_b_i=ml_b_i, ml_b_f=ml_b_f,
             ml_w_proj=ml_w_proj, w_out=w_out, ffn_w_up=ffn_w_up, ffn_w_gate=ffn_w_gate,
             ffn_conv_w=ffn_conv_w, ffn_conv_b=ffn_conv_b, ffn_w_down=ffn_w_down)
    bsz, s, _ = x.shape
    assert bsz == 1 and s % (S5_BLOCK * SUBLANES) == 0
    t_s5, t_mix, t_ffn, t_ff = _tiles(s)
    h = x[0]
    for l in range(w_in.shape[0]):
        mixer_w, s5_w, s5_prm, ffn_w = _pack_layer(l, p)
        u_t = _s5_in_proj(h, mixer_w["gpre"], s5_w, t_s5)
        y_t = _s5_core(u_t, s5_prm)
        h = _mixer(h, y_t, mixer_w, t_mix)
        h = _ffn(h, ffn_w, t_ffn, t_ff)
    return h[None]
```

```python
import functools
import math

import jax
import jax.numpy as jnp
from jax import lax
from jax.experimental import pallas as pl
from jax.experimental.pallas import tpu as pltpu

F32 = jnp.float32
BF16 = jnp.bfloat16

D_MODEL = 1024
CHUNK = 64
NORM_EPS = 1e-6
BRANCH = 512
HEADS = 4
HEAD_DIM = BRANCH // HEADS
GLA_RANK = 16
GLA_GATE_NORM = 16.0
S5_GROUP_CH = 16
S5_GROUPS = BRANCH // S5_GROUP_CH
S5_STATE = 64
S5_BLOCK = 128
ML_CONV = 4
D_FF = 2816
FFN_CONV = 3
LANES = 128
SUBLANES = 8

VMEM_LIMIT = 56 * 1024 * 1024


def _rms_norm(x, gain):
    return x * lax.rsqrt(jnp.mean(x * x, axis=-1, keepdims=True) + NORM_EPS) * gain


def _dot(a, b):
    return jnp.dot(a, b, preferred_element_type=F32)


def _dot_nt(a, b):
    return lax.dot_general(a, b, (((1,), (1,)), ((), ())), preferred_element_type=F32)


def _log_sigmoid(x):
    return jnp.minimum(x, 0.0) - jnp.log1p(jnp.exp(-jnp.abs(x)))


def _sigmoid(x):
    return 1.0 / (1.0 + jnp.exp(-x))


def _silu(x):
    return x * _sigmoid(x)


def _gelu_tanh(x):
    c = math.sqrt(2.0 / math.pi)
    return x * (0.5 * (1.0 + jnp.tanh(c * (x + 0.044715 * (x * x * x)))))


def _full_spec(shape):
    zeros = (0,) * len(shape)
    return pl.BlockSpec(shape, lambda *_: zeros, pipeline_mode=pl.Buffered(1))


def _s5_in_kernel(x_ref, gain_ref, w_ref, o_ref):
    u = _rms_norm(x_ref[...], gain_ref[...]).astype(BF16)
    o_ref[...] = _dot_nt(w_ref[...], u).astype(BF16)


def _s5_in_proj(x, gain, w_t, tile):
    s = x.shape[0]
    return pl.pallas_call(
        _s5_in_kernel,
        grid=(s // tile,),
        in_specs=[pl.BlockSpec((tile, D_MODEL), lambda i: (i, 0)),
                  _full_spec((1, D_MODEL)),
                  _full_spec((BRANCH, D_MODEL))],
        out_specs=pl.BlockSpec((BRANCH, tile), lambda i: (0, i)),
        out_shape=jax.ShapeDtypeStruct((BRANCH, s), BF16),
        compiler_params=pltpu.CompilerParams(dimension_semantics=("arbitrary",),
                                             vmem_limit_bytes=VMEM_LIMIT),
        name="s5_in_proj",
    )(x, gain, w_t)


def _cmul(xr, xi, yr, yi):
    return xr * yr - xi * yi, xr * yi + xi * yr


def _s5_kernel(u_ref, are_c_ref, aim_c_ref, are_r_ref, aim_r_ref, ldt_ref,
               b_re_ref, b_im_ref, bt_re_ref, bt_im_ref, c_re_ref, c_im_ref, ct_re_ref, ct_im_ref, d_ref,
               o_ref, k_sc, ta_sc, tb_sc, acc_sc, win_sc, m_sc):
    nch, nblk, blk = u_ref.shape
    half = S5_STATE
    blk_f = float(blk)
    dt = jnp.exp(ldt_ref[...])

    def zoh_factor(are, aim, abr, abi):
        den = are * are + aim * aim
        zr = abr - 1.0
        return (zr * are + abi * aim) / den, (abi * are - zr * aim) / den

    are_r, aim_r = are_r_ref[...], aim_r_ref[...]
    rho_r, th_r = dt * are_r, dt * aim_r
    mag_r = jnp.exp(rho_r)
    fr_r, fi_r = zoh_factor(are_r, aim_r, mag_r * jnp.cos(th_r), mag_r * jnp.sin(th_r))
    bbr, bbi = _cmul(fr_r, fi_r, bt_re_ref[...], bt_im_ref[...])

    are_c, aim_c = are_c_ref[...], aim_c_ref[...]
    rho_c, th_c = dt * are_c, dt * aim_c
    lag_l = lax.broadcasted_iota(jnp.int32, (half, blk), 1).astype(F32)
    pmag = jnp.exp(rho_c * lag_l)
    p_re, p_im = pmag * jnp.cos(th_c * lag_l), pmag * jnp.sin(th_c * lag_l)
    abr_c, abi_c = p_re[:, 1:2], p_im[:, 1:2]
    fr_c, fi_c = zoh_factor(are_c, aim_c, abr_c, abi_c)
    bcr, bci = _cmul(fr_c, fi_c, b_re_ref[...], b_im_ref[...])

    c_re, c_im = c_re_ref[...], c_im_ref[...]
    cb_rows = []
    for hi in range(nch):
        cbr, cbi = _cmul(c_re, c_im, bbr[hi:hi + 1, :], bbi[hi:hi + 1, :])
        cb_rows.append(jnp.concatenate([cbr, cbi], axis=1))
    cb = jnp.concatenate(cb_rows, axis=0)
    pstack = jnp.concatenate([p_re, -p_im], axis=0)
    kern = jnp.dot(cb, pstack, preferred_element_type=F32, precision=lax.Precision.HIGHEST)
    rr = lax.broadcasted_iota(jnp.int32, (nch * nch, blk), 0)
    ll = lax.broadcasted_iota(jnp.int32, (nch * nch, blk), 1)
    d_col = jnp.concatenate([d_ref[...]] * nch, axis=0)
    diag = ((rr >> 4) == (rr & (nch - 1))) & (ll == 0)
    k_sc[...] = kern + jnp.where(diag, d_col, 0.0)

    lag_s = blk_f - 1.0 - lag_l
    qmag = jnp.exp(rho_c * lag_s)
    q_re, q_im = qmag * jnp.cos(th_c * lag_s), qmag * jnp.sin(th_c * lag_s)
    for hi in range(nch):
        wr, wi = _cmul(q_re, q_im, bcr[:, hi:hi + 1], bci[:, hi:hi + 1])
        win_sc[:, hi * blk:(hi + 1) * blk] = jnp.concatenate([wr, wi], axis=0).astype(BF16)

    p1_re, p1_im = _cmul(p_re, p_im, abr_c, abi_c)
    ct_re, ct_im = ct_re_ref[...], ct_im_ref[...]
    for ho in range(nch):
        mr, mi = _cmul(p1_re, p1_im, ct_re[:, ho:ho + 1], ct_im[:, ho:ho + 1])
        m_sc[:, ho * blk:(ho + 1) * blk] = jnp.concatenate([mr, -mi], axis=0).astype(BF16)

    ucat = jnp.concatenate([u_ref[h] for h in range(nch)], axis=1)

    x = _dot_nt(ucat, win_sc[...])
    wmag = jnp.exp(rho_r * blk_f)
    w_re, w_im = wmag * jnp.cos(th_r * blk_f), wmag * jnp.sin(th_r * blk_f)
    row = lax.broadcasted_iota(jnp.int32, (nblk, 2 * half), 0)

    def shift_rows(a, k):
        return jnp.where(row >= k, pltpu.roll(a, k, 0), 0.0)

    def state_mul(a, wr, wi):
        return (a * jnp.concatenate([wr, wr], axis=1)
                + pltpu.roll(a, half, 1) * jnp.concatenate([-wi, wi], axis=1))

    step = 1
    while step < nblk:
        x = x + state_mul(shift_rows(x, step), w_re, w_im)
        w_re, w_im = _cmul(w_re, w_im, w_re, w_im)
        step *= 2
    acc_sc[...] = _dot(shift_rows(x, 1).astype(BF16), m_sc[...])

    srow = lax.broadcasted_iota(jnp.int32, (blk, blk), 0)
    lcol = lax.broadcasted_iota(jnp.int32, (blk, blk), 1)
    lower = lcol >= srow

    def build(pair, slab):
        for sub in range(2):
            for ho in range(nch):
                krow = k_sc[pl.ds((2 * pair + sub) * nch + ho, 1), :]
                tile = pltpu.roll(jnp.broadcast_to(krow, (blk, blk)), 0, 1, stride=1, stride_axis=0)
                slab[sub * blk:(sub + 1) * blk, ho * blk:(ho + 1) * blk] = (
                    jnp.where(lower, tile, 0.0).astype(BF16))

    def apply(pair, slab):
        u_pair = jnp.concatenate([u_ref[2 * pair], u_ref[2 * pair + 1]], axis=1)
        acc_sc[...] += _dot(u_pair, slab[...])

    npair = nch // 2
    build(0, ta_sc)

    def two_pairs(q, carry):
        build(2 * q + 1, tb_sc)
        apply(2 * q, ta_sc)
        build(jnp.minimum(2 * q + 2, npair - 1), ta_sc)
        apply(2 * q + 1, tb_sc)
        return carry

    lax.fori_loop(0, npair // 2, two_pairs, 0)
    for ho in range(nch):
        o_ref[ho] = acc_sc[:, ho * blk:(ho + 1) * blk]


def _s5_core(u_t, prm):
    s = u_t.shape[1]
    nblk = s // S5_BLOCK
    u4 = u_t.reshape(S5_GROUPS, S5_GROUP_CH, nblk, S5_BLOCK)

    def gspec(*dims):
        zeros = (0,) * len(dims)
        return pl.BlockSpec((None,) + dims, lambda g: (g,) + zeros)

    width = S5_GROUP_CH * S5_BLOCK
    out = pl.pallas_call(
        _s5_kernel,
        grid=(S5_GROUPS,),
        in_specs=[gspec(S5_GROUP_CH, nblk, S5_BLOCK),
                  gspec(S5_STATE, 1), gspec(S5_STATE, 1), gspec(1, S5_STATE), gspec(1, S5_STATE),
                  gspec(1, 1),
                  gspec(S5_STATE, S5_GROUP_CH), gspec(S5_STATE, S5_GROUP_CH),
                  gspec(S5_GROUP_CH, S5_STATE), gspec(S5_GROUP_CH, S5_STATE),
                  gspec(S5_GROUP_CH, S5_STATE), gspec(S5_GROUP_CH, S5_STATE),
                  gspec(S5_STATE, S5_GROUP_CH), gspec(S5_STATE, S5_GROUP_CH),
                  gspec(S5_GROUP_CH, 1)],
        out_specs=gspec(S5_GROUP_CH, nblk, S5_BLOCK),
        out_shape=jax.ShapeDtypeStruct(u4.shape, F32),
        scratch_shapes=[pltpu.VMEM((S5_GROUP_CH * S5_GROUP_CH, S5_BLOCK), F32),
                        pltpu.VMEM((2 * S5_BLOCK, width), BF16),
                        pltpu.VMEM((2 * S5_BLOCK, width), BF16),
                        pltpu.VMEM((nblk, width), F32),
                        pltpu.VMEM((2 * S5_STATE, width), BF16),
                        pltpu.VMEM((2 * S5_STATE, width), BF16)],
        compiler_params=pltpu.CompilerParams(dimension_semantics=("arbitrary",),
                                             vmem_limit_bytes=VMEM_LIMIT),
        name="s5_core",
    )(u4, *prm)
    return out.reshape(BRANCH, s)


def _s5_params(a_re, a_im, log_dt, b_re, b_im, c_re, c_im, d):
    g = S5_GROUPS
    return (a_re.reshape(g, S5_STATE, 1), a_im.reshape(g, S5_STATE, 1),
            a_re.reshape(g, 1, S5_STATE), a_im.reshape(g, 1, S5_STATE),
            log_dt.reshape(g, 1, 1),
            b_re, b_im,
            jnp.swapaxes(b_re, 1, 2), jnp.swapaxes(b_im, 1, 2),
            c_re, c_im,
            jnp.swapaxes(c_re, 1, 2), jnp.swapaxes(c_im, 1, 2),
            d.reshape(g, S5_GROUP_CH, 1))


_C_GQ, _C_GK, _C_GV, _C_GR = 0, 512, 1024, 1536
_C_MQ, _C_MK, _C_MV, _C_MO = 2048, 2560, 3072, 3584
_C_G0 = 4096
_C_LR = _C_G0 + 3 * D_MODEL
_C_IF = _C_LR + LANES
_W_COLS = _C_IF + LANES

GROUP = 4 * CHUNK
_LOG2_CHUNK = CHUNK.bit_length() - 1
_LOG2_HEAD_DIM = HEAD_DIM.bit_length() - 1
_GATE_SLICE = 256
_N_GATE_SLICES = 3 * D_MODEL // _GATE_SLICE


def _group_masks():
    r = lax.broadcasted_iota(jnp.int32, (GROUP, GROUP), 0)
    c = lax.broadcasted_iota(jnp.int32, (GROUP, GROUP), 1)
    causal = ((r >> _LOG2_CHUNK) == (c >> _LOG2_CHUNK)) & (r >= c)
    rr = lax.broadcasted_iota(jnp.int32, (4 * HEAD_DIM, GROUP), 0)
    cc = lax.broadcasted_iota(jnp.int32, (4 * HEAD_DIM, GROUP), 1)
    stacked = (rr >> _LOG2_HEAD_DIM) == (cc >> _LOG2_CHUNK)
    return causal, stacked


def _cumsum_chunks(tri, a):
    hi = a.astype(BF16)
    mid = (a - hi.astype(F32)).astype(BF16)
    return _dot(tri, hi) + _dot(tri, mid)


def _chunk_rows(c):
    return slice(c * CHUNK, (c + 1) * CHUNK)


def _per_chunk_sums(x_t, stacked, rhs):
    lhs = jnp.where(stacked, jnp.concatenate([x_t] * 4, axis=0), 0.0).astype(BF16)
    return _dot(lhs, rhs)


def _gla_group(q, k, v, la, states, causal, tri, stacked, filler):
    hd = HEAD_DIM
    b = _cumsum_chunks(tri, la)
    last = [b[c * CHUNK + CHUNK - 1:(c + 1) * CHUNK, :] for c in range(4)]
    b_last = jnp.concatenate([jnp.broadcast_to(r, (CHUNK, BRANCH)) for r in last], axis=0)
    q_dec = (q * (hd ** -0.5) * jnp.exp(b)).astype(BF16)
    k_inv = (k * jnp.exp(-b)).astype(BF16)
    k_end = (k * jnp.exp(b_last - b)).astype(BF16)
    vb = v.astype(BF16)
    decay = [jnp.exp(r) for r in last]
    outs, new_states = [], []
    for h in range(HEADS):
        sl = slice(h * hd, (h + 1) * hd)
        filler()
        scores = jnp.where(causal, _dot_nt(q_dec[:, sl], k_inv[:, sl]), 0.0).astype(BF16)
        intra = _dot(scores, vb[:, sl])
        kv = _per_chunk_sums(v[:, sl].T, stacked, k_end[:, sl])
        st = states[h]
        inter = []
        for c in range(4):
            inter.append(_dot_nt(q_dec[_chunk_rows(c), sl], st.astype(BF16)))
            st = st * decay[c][:, sl] + kv[c * hd:(c + 1) * hd, :]
        new_states.append(st)
        outs.append(intra + jnp.concatenate(inter, axis=0))
    return jnp.concatenate(outs, axis=1), new_states


def _mlstm_group(q, k, v, gates, states, m_run, causal, tri, stacked, filler):
    hd = HEAD_DIM
    cs = _cumsum_chunks(tri, gates)
    log_i = pltpu.roll(gates, HEADS, 1)
    w_rows, m_prev_rows, scale = [], [], []
    for c in range(4):
        rows = _chunk_rows(c)
        b_last = cs[c * CHUNK + CHUNK - 1:(c + 1) * CHUNK, :]
        g_end = b_last - cs[rows, :] + log_i[rows, :]
        m_new = jnp.maximum(b_last + m_run, jnp.max(g_end, axis=0, keepdims=True))
        w_rows.append(jnp.exp(g_end - m_new))
        scale.append(jnp.exp(b_last + m_run - m_new))
        m_prev_rows.append(jnp.broadcast_to(m_run, (CHUNK, LANES)))
        m_run = m_new
    w_all = jnp.concatenate(w_rows, axis=0)
    m_inter = cs + jnp.concatenate(m_prev_rows, axis=0)
    lane = lax.broadcasted_iota(jnp.int32, (GROUP, LANES), 1)
    z_t = jnp.where(lane < HEADS, gates, cs).T
    ones = jnp.ones((GROUP, hd), BF16)
    outs, new_states = [], []
    for h in range(HEADS):
        sl = slice(h * hd, (h + 1) * hd)
        col = HEADS + h
        filler()
        log_w = jnp.where(causal, cs[:, col:col + 1] - z_t[col:col + 1, :] + z_t[h:h + 1, :], -jnp.inf)
        m_int = m_inter[:, col:col + 1]
        m = jnp.maximum(m_int, jnp.max(log_w, axis=-1, keepdims=True))
        s_inter = jnp.exp(m_int - m)
        qb = q[:, sl].astype(BF16)
        kh = k[:, sl]
        v_aug = jnp.concatenate([v[:, sl].astype(BF16), ones], axis=1)
        qk = (_dot_nt(qb, kh.astype(BF16)) * jnp.exp(log_w - m)).astype(BF16)
        intra = _dot(qk, v_aug)
        kw_t = (kh * w_all[:, col:col + 1]).T
        kv = _per_chunk_sums(kw_t, stacked, v_aug)
        st = states[h]
        inter = []
        for c in range(4):
            inter.append(_dot(qb[_chunk_rows(c), :], st.astype(BF16)))
            st = scale[c][:, col:col + 1] * st + kv[c * hd:(c + 1) * hd, :]
        new_states.append(st)
        tot = s_inter * jnp.concatenate(inter, axis=0) + intra
        outs.append(tot[:, :hd] / jnp.maximum(jnp.abs(tot[:, hd:]), jnp.exp(-m)))
    return jnp.concatenate(outs, axis=1), new_states, m_run


def _mixer_kernel(x_ref, yst_ref, gpre_ref, gpost_ref, win_ref,
                  wgk_ref, bgk_ref, gnorm_ref, wgp_ref,
                  wglu_ref, bglu_ref, wsp_ref,
                  cw_ref, cb_ref, bif_ref, wmp_ref, wout_ref,
                  o_ref,
                  gst_sc, mqk_sc, mc_sc, mm_sc, carry_sc):
    tm = x_ref.shape[0]
    ngroup = tm // GROUP
    hd = HEAD_DIM

    @pl.when(pl.program_id(0) == 0)
    def _init():
        gst_sc[...] = jnp.zeros_like(gst_sc)
        mc_sc[...] = jnp.zeros_like(mc_sc)
        mm_sc[...] = jnp.zeros_like(mm_sc)
        carry_sc[...] = jnp.zeros_like(carry_sc)

    x = x_ref[...]
    u = _rms_norm(x, gpre_ref[...]).astype(BF16)

    def proj(c0, width):
        return _dot(u, win_ref[:, c0:c0 + width])

    causal, stacked = _group_masks()
    tri = causal.astype(BF16)

    done, queue = {}, []

    def filler(n=1):
        for _ in range(n):
            if queue:
                name, fn = queue.pop(0)
                done[name] = fn()

    def need(name):
        while name not in done:
            filler()
        return done[name]

    def s5_tail():
        ys = _gelu_tanh(yst_ref[...].T)
        glu = _sigmoid(_dot(ys.astype(BF16), wglu_ref[...]) + bglu_ref[...])
        return _dot((ys * glu).astype(BF16), wsp_ref[...])

    queue += [("mq", lambda: proj(_C_MQ, BRANCH)), ("mk", lambda: proj(_C_MK, BRANCH)),
              ("mv", lambda: proj(_C_MV, BRANCH)),
              ("pre_if", lambda: proj(_C_IF, LANES) + bif_ref[...]),
              ("mo", lambda: _sigmoid(proj(_C_MO, BRANCH))),
              ("g_r", lambda: _silu(proj(_C_GR, BRANCH)))]
    queue += [("gate%d" % i, functools.partial(
        lambda c0: _sigmoid(proj(c0, _GATE_SLICE)), _C_G0 + i * _GATE_SLICE))
        for i in range(_N_GATE_SLICES)]
    queue.insert(8, ("y_s5", s5_tail))

    gq, gk, gv = proj(_C_GQ, BRANCH), proj(_C_GK, BRANCH), proj(_C_GV, BRANCH)
    g_lr = proj(_C_LR, LANES).astype(BF16)
    log_a = _log_sigmoid(_dot(g_lr, wgk_ref[...]) + bgk_ref[...]) / GLA_GATE_NORM
    states = [gst_sc[h] for h in range(HEADS)]
    o_groups = []
    for g in range(ngroup):
        r = slice(g * GROUP, (g + 1) * GROUP)
        o_g, states = _gla_group(gq[r], gk[r], gv[r], log_a[r], states, causal, tri, stacked, filler)
        o_groups.append(o_g)
    for h in range(HEADS):
        gst_sc[h] = states[h]
    o = jnp.concatenate(o_groups, axis=0)
    gnorm = gnorm_ref[...]
    o_heads = [_rms_norm(o[:, h * hd:(h + 1) * hd], gnorm) for h in range(HEADS)]
    o_gated = (jnp.concatenate(o_heads, axis=1) * need("g_r")).astype(BF16)
    queue.insert(0, ("y_gla", lambda: _dot(o_gated, wgp_ref[...])))

    qk_pre = jnp.concatenate([need("mq"), need("mk")], axis=1)
    mqk_sc[0:SUBLANES, :] = carry_sc[...]
    mqk_sc[SUBLANES:, :] = qk_pre
    carry_sc[...] = qk_pre[tm - SUBLANES:, :]
    cw = cw_ref[...]
    conv = cb_ref[...] + cw[ML_CONV - 1:ML_CONV, :] * qk_pre
    for j in range(1, ML_CONV):
        conv = conv + cw[ML_CONV - 1 - j:ML_CONV - j, :] * mqk_sc[SUBLANES - j:SUBLANES - j + tm, :]
    qk_act = _silu(conv)
    mq = qk_act[:, 0:BRANCH]
    mk = qk_act[:, BRANCH:] * (hd ** -0.5)
    mv = need("mv")
    pre_if = need("pre_if")
    lane = lax.broadcasted_iota(jnp.int32, (tm, LANES), 1)
    gates = jnp.where(lane < HEADS, pre_if, _log_sigmoid(pre_if))
    states = [mc_sc[h] for h in range(HEADS)]
    m_run = mm_sc[0:1, :]
    h_groups = []
    for g in range(ngroup):
        r = slice(g * GROUP, (g + 1) * GROUP)
        h_g, states, m_run = _mlstm_group(mq[r], mk[r], mv[r], gates[r], states, m_run,
                                          causal, tri, stacked, functools.partial(filler, 2))
        h_groups.append(h_g)
    for h in range(HEADS):
        mc_sc[h] = states[h]
    mm_sc[...] = jnp.broadcast_to(m_run, mm_sc.shape)
    hid = (need("mo") * jnp.concatenate(h_groups, axis=0)).astype(BF16)
    y_ml = _dot(hid, wmp_ref[...])

    per_branch = D_MODEL // _GATE_SLICE
    branch_gate = [jnp.concatenate([need("gate%d" % (b * per_branch + i)) for i in range(per_branch)],
                                   axis=1) for b in range(3)]
    merged = branch_gate[0] * need("y_gla") + branch_gate[1] * need("y_s5") + branch_gate[2] * y_ml
    mix = _dot(merged.astype(BF16), wout_ref[...])
    o_ref[...] = x + _rms_norm(mix, gpost_ref[...])


def _mixer(x, ys_t, w, tile):
    s = x.shape[0]
    weights = (w["gpre"], w["gpost"], w["win"], w["wgk"], w["bgk"], w["gnorm"], w["wgp"],
               w["wglu"], w["bglu"], w["wsp"], w["cw"], w["cb"], w["bif"], w["wmp"], w["wout"])
    return pl.pallas_call(
        _mixer_kernel,
        grid=(s // tile,),
        in_specs=[pl.BlockSpec((tile, D_MODEL), lambda i: (i, 0)),
                  pl.BlockSpec((BRANCH, tile), lambda i: (0, i))]
                 + [_full_spec(a.shape) for a in weights],
        out_specs=pl.BlockSpec((tile, D_MODEL), lambda i: (i, 0)),
        out_shape=jax.ShapeDtypeStruct((s, D_MODEL), F32),
        scratch_shapes=[pltpu.VMEM((HEADS, HEAD_DIM, HEAD_DIM), F32),
                        pltpu.VMEM((tile + SUBLANES, 2 * BRANCH), F32),
                        pltpu.VMEM((HEADS, HEAD_DIM, 2 * HEAD_DIM), F32),
                        pltpu.VMEM((SUBLANES, LANES), F32),
                        pltpu.VMEM((SUBLANES, 2 * BRANCH), F32)],
        compiler_params=pltpu.CompilerParams(dimension_semantics=("arbitrary",),
                                             vmem_limit_bytes=VMEM_LIMIT),
        name="mixer",
    )(x, ys_t, *weights)


_FFN_ROW_BLOCKS = 2

def _ffn_kernel(x_ref, gpre_ref, gpost_ref, wup_ref, wgate_ref, cw_ref, cb_ref, wdown_ref,
                o_ref, u_sc, acc_sc, a_sc, carry_sc):
    i, f = pl.program_id(0), pl.program_id(1)
    tm = x_ref.shape[0]

    @pl.when(f == 0)
    def _norm():
        u_sc[...] = _rms_norm(x_ref[...], gpre_ref[...]).astype(BF16)
        acc_sc[...] = jnp.zeros_like(acc_sc)

    @pl.when(i == 0)
    def _zero_carry():
        carry_sc[f] = jnp.zeros(carry_sc.shape[1:], F32)

    a_sc[0:SUBLANES, :] = carry_sc[f]
    cw = cw_ref[...]
    rb = tm // _FFN_ROW_BLOCKS
    for r in range(_FFN_ROW_BLOCKS):
        rows = slice(r * rb, (r + 1) * rb)
        u = u_sc[rows, :]
        a = _dot(u, wup_ref[...])
        gate = _dot(u, wgate_ref[...])
        a_sc[SUBLANES + r * rb:SUBLANES + (r + 1) * rb, :] = a
        conv = cb_ref[...] + cw[FFN_CONV - 1:FFN_CONV, :] * a
        for j in range(1, FFN_CONV):
            lo = SUBLANES + r * rb - j
            conv = conv + cw[FFN_CONV - 1 - j:FFN_CONV - j, :] * a_sc[lo:lo + rb, :]
        h = (_gelu_tanh(conv) * gate).astype(BF16)
        acc_sc[rows, :] += _dot(h, wdown_ref[...])

    carry_sc[f] = a_sc[tm:tm + SUBLANES, :]

    @pl.when(f == pl.num_programs(1) - 1)
    def _finish():
        o_ref[...] = x_ref[...] + _rms_norm(acc_sc[...], gpost_ref[...])


def _ffn(x, w, tile, ftile):
    s = x.shape[0]
    nf = D_FF // ftile
    return pl.pallas_call(
        _ffn_kernel,
        grid=(s // tile, nf),
        in_specs=[pl.BlockSpec((tile, D_MODEL), lambda i, f: (i, 0)),
                  pl.BlockSpec((1, D_MODEL), lambda i, f: (0, 0)),
                  pl.BlockSpec((1, D_MODEL), lambda i, f: (0, 0)),
                  pl.BlockSpec((D_MODEL, ftile), lambda i, f: (0, f)),
                  pl.BlockSpec((D_MODEL, ftile), lambda i, f: (0, f)),
                  pl.BlockSpec((FFN_CONV, ftile), lambda i, f: (0, f)),
                  pl.BlockSpec((1, ftile), lambda i, f: (0, f)),
                  pl.BlockSpec((ftile, D_MODEL), lambda i, f: (f, 0))],
        out_specs=pl.BlockSpec((tile, D_MODEL), lambda i, f: (i, 0)),
        out_shape=jax.ShapeDtypeStruct((s, D_MODEL), F32),
        scratch_shapes=[pltpu.VMEM((tile, D_MODEL), BF16),
                        pltpu.VMEM((tile, D_MODEL), F32),
                        pltpu.VMEM((tile + SUBLANES, ftile), F32),
                        pltpu.VMEM((nf, SUBLANES, ftile), F32)],
        compiler_params=pltpu.CompilerParams(dimension_semantics=("arbitrary", "arbitrary"),
                                             vmem_limit_bytes=VMEM_LIMIT),
        name="conv_ffn",
    )(x, w["gpre"], w["gpost"], w["wup"], w["wgate"], w["cw"], w["cb"], w["wdown"])


def _pad_cols(a, width):
    return jnp.pad(a, ((0, 0), (0, width - a.shape[1])))


def _pack_layer(l, p):
    w_in = p["w_in"][l].astype(BF16)
    sp = (0, 512, 1024, 1536, 1552, 2064, 2576, 3088, 3600, 4112, 4116, 4120, 4632, 7704)
    (g_q, g_k, g_v, g_lr, g_r, s_u, m_q, m_k, m_v, m_i, m_f, m_o, gates) = [
        w_in[:, sp[j]:sp[j + 1]] for j in range(13)]
    win = jnp.concatenate(
        [g_q, g_k, g_v, g_r, m_q, m_k, m_v, m_o, gates,
         _pad_cols(g_lr, LANES), _pad_cols(jnp.concatenate([m_i, m_f], axis=1), LANES)],
        axis=1)
    row = lambda a: a.reshape(1, -1)
    mixer = dict(
        gpre=row(p["norm_mix_pre"][l]), gpost=row(p["norm_mix_post"][l]), win=win,
        wgk=jnp.pad(p["gla_w_gk"][l], ((0, LANES - GLA_RANK), (0, 0))).astype(BF16),
        bgk=row(p["gla_b_gk"][l]), gnorm=row(p["gla_norm"][l]),
        wgp=p["gla_w_proj"][l].astype(BF16),
        wglu=p["s5_w_glu"][l].astype(BF16), bglu=row(p["s5_b_glu"][l]),
        wsp=p["s5_w_proj"][l].astype(BF16),
        cw=p["ml_conv_w"][l], cb=row(p["ml_conv_b"][l]),
        bif=_pad_cols(row(jnp.concatenate([p["ml_b_i"][l], p["ml_b_f"][l]])), LANES),
        wmp=p["ml_w_proj"][l].astype(BF16), wout=p["w_out"][l].astype(BF16))
    s5_w = s_u.T
    s5_prm = _s5_params(p["s5_a_re"][l], p["s5_a_im"][l], p["s5_log_dt"][l], p["s5_b_re"][l],
                        p["s5_b_im"][l], p["s5_c_re"][l], p["s5_c_im"][l], p["s5_d"][l])
    ffn = dict(gpre=row(p["norm_ffn_pre"][l]), gpost=row(p["norm_ffn_post"][l]),
               wup=p["ffn_w_up"][l].astype(BF16), wgate=p["ffn_w_gate"][l].astype(BF16),
               cw=p["ffn_conv_w"][l], cb=row(p["ffn_conv_b"][l]),
               wdown=p["ffn_w_down"][l].astype(BF16))
    return mixer, s5_w, s5_prm, ffn


def _tiles(s):
    return min(1024, s), min(512, s), min(1024, s), D_FF // 2


def kernel(x, norm_mix_pre, norm_mix_post, norm_ffn_pre, norm_ffn_post, w_in, gla_w_gk, gla_b_gk, gla_norm, gla_w_proj, s5_a_re, s5_a_im, s5_log_dt, s5_b_re, s5_b_im, s5_c_re, s5_c_im, s5_d, s5_w_glu, s5_b_glu, s5_w_proj, ml_conv_w, ml_conv_b, ml_b_i, ml_b_f, ml_w_proj, w_out, ffn_w_up, ffn_w_gate, ffn_conv_w, ffn_conv_b, ffn_w_down):
    p = dict(norm_mix_pre=norm_mix_pre, norm_mix_post=norm_mix_post, norm_ffn_pre=norm_ffn_pre,
             norm_ffn_post=norm_ffn_post, w_in=w_in, gla_w_gk=gla_w_gk, gla_b_gk=gla_b_gk,
             gla_norm=gla_norm, gla_w_proj=gla_w_proj, s5_a_re=s5_a_re, s5_a_im=s5_a_im,
             s5_log_dt=s5_log_dt, s5_b_re=s5_b_re, s5_b_im=s5_b_im, s5_c_re=s5_c_re,
             s5_c_im=s5_c_im, s5_d=s5_d, s5_w_glu=s5_w_glu, s5_b_glu=s5_b_glu, s5_w_proj=s5_w_proj,
             ml_conv_w=ml_conv_w, ml_conv_b=ml_conv_b, ml_b_i=ml_b_i, ml_b_f=ml_b_f,
             ml_w_proj=ml_w_proj, w_out=w_out, ffn_w_up=ffn_w_up, ffn_w_gate=ffn_w_gate,
             ffn_conv_w=ffn_conv_w, ffn_conv_b=ffn_conv_b, ffn_w_down=ffn_w_down)
    bsz, s, _ = x.shape
    assert bsz == 1 and s % (S5_BLOCK * SUBLANES) == 0
    t_s5, t_mix, t_ffn, t_ff = _tiles(s)
    h = x[0]
    for l in range(w_in.shape[0]):
        mixer_w, s5_w, s5_prm, ffn_w = _pack_layer(l, p)
        u_t = _s5_in_proj(h, mixer_w["gpre"], s5_w, t_s5)
        y_t = _s5_core(u_t, s5_prm)
        h = _mixer(h, y_t, mixer_w, t_mix)
        h = _ffn(h, ffn_w, t_ffn, t_ff)
    return h[None]
```

```python
import functools
import math

import jax
import jax.numpy as jnp
from jax import lax
from jax.experimental import pallas as pl
from jax.experimental.pallas import tpu as pltpu

F32 = jnp.float32
BF16 = jnp.bfloat16

D_MODEL = 1024
CHUNK = 64
NORM_EPS = 1e-6
BRANCH = 512
HEADS = 4
HEAD_DIM = BRANCH // HEADS
GLA_RANK = 16
GLA_GATE_NORM = 16.0
S5_GROUP_CH = 16
S5_GROUPS = BRANCH // S5_GROUP_CH
S5_STATE = 64
S5_BLOCK = 128
ML_CONV = 4
D_FF = 2816
FFN_CONV = 3
LANES = 128
SUBLANES = 8

VMEM_LIMIT = 56 * 1024 * 1024


def _rms_norm(x, gain):
    return x * lax.rsqrt(jnp.mean(x * x, axis=-1, keepdims=True) + NORM_EPS) * gain


def _dot(a, b):
    return jnp.dot(a, b, preferred_element_type=F32)


def _dot_nt(a, b):
    return lax.dot_general(a, b, (((1,), (1,)), ((), ())), preferred_element_type=F32)


def _log_sigmoid(x):
    return jnp.minimum(x, 0.0) - jnp.log1p(jnp.exp(-jnp.abs(x)))


def _sigmoid(x):
    return 1.0 / (1.0 + jnp.exp(-x))


def _silu(x):
    return x * _sigmoid(x)


def _gelu_tanh(x):
    c = math.sqrt(2.0 / math.pi)
    return x * (0.5 * (1.0 + jnp.tanh(c * (x + 0.044715 * (x * x * x)))))


def _full_spec(shape):
    zeros = (0,) * len(shape)
    return pl.BlockSpec(shape, lambda *_: zeros, pipeline_mode=pl.Buffered(1))


def _s5_in_kernel(x_ref, gain_ref, w_ref, o_ref):
    u = _rms_norm(x_ref[...], gain_ref[...]).astype(BF16)
    o_ref[...] = _dot_nt(w_ref[...], u).astype(BF16)


def _s5_in_proj(x, gain, w_t, tile):
    s = x.shape[0]
    return pl.pallas_call(
        _s5_in_kernel,
        grid=(s // tile,),
        in_specs=[pl.BlockSpec((tile, D_MODEL), lambda i: (i, 0)),
                  _full_spec((1, D_MODEL)),
                  _full_spec((BRANCH, D_MODEL))],
        out_specs=pl.BlockSpec((BRANCH, tile), lambda i: (0, i)),
        out_shape=jax.ShapeDtypeStruct((BRANCH, s), BF16),
        compiler_params=pltpu.CompilerParams(dimension_semantics=("arbitrary",),
                                             vmem_limit_bytes=VMEM_LIMIT),
        name="s5_in_proj",
    )(x, gain, w_t)


def _cmul(xr, xi, yr, yi):
    return xr * yr - xi * yi, xr * yi + xi * yr


def _s5_kernel(u_ref, are_c_ref, aim_c_ref, are_r_ref, aim_r_ref, ldt_ref,
               b_re_ref, b_im_ref, bt_re_ref, bt_im_ref, c_re_ref, c_im_ref, ct_re_ref, ct_im_ref, d_ref,
               o_ref, k_sc, ta_sc, tb_sc, acc_sc, win_sc, m_sc):
    nch, nblk, blk = u_ref.shape
    half = S5_STATE
    blk_f = float(blk)
    dt = jnp.exp(ldt_ref[...])

    def zoh_factor(are, aim, abr, abi):
        den = are * are + aim * aim
        zr = abr - 1.0
        return (zr * are + abi * aim) / den, (abi * are - zr * aim) / den

    are_r, aim_r = are_r_ref[...], aim_r_ref[...]
    rho_r, th_r = dt * are_r, dt * aim_r
    mag_r = jnp.exp(rho_r)
    fr_r, fi_r = zoh_factor(are_r, aim_r, mag_r * jnp.cos(th_r), mag_r * jnp.sin(th_r))
    bbr, bbi = _cmul(fr_r, fi_r, bt_re_ref[...], bt_im_ref[...])

    are_c, aim_c = are_c_ref[...], aim_c_ref[...]
    rho_c, th_c = dt * are_c, dt * aim_c
    lag_l = lax.broadcasted_iota(jnp.int32, (half, blk), 1).astype(F32)
    pmag = jnp.exp(rho_c * lag_l)
    p_re, p_im = pmag * jnp.cos(th_c * lag_l), pmag * jnp.sin(th_c * lag_l)
    abr_c, abi_c = p_re[:, 1:2], p_im[:, 1:2]
    fr_c, fi_c = zoh_factor(are_c, aim_c, abr_c, abi_c)
    bcr, bci = _cmul(fr_c, fi_c, b_re_ref[...], b_im_ref[...])

    c_re, c_im = c_re_ref[...], c_im_ref[...]
    cb_rows = []
    for hi in range(nch):
        cbr, cbi = _cmul(c_re, c_im, bbr[hi:hi + 1, :], bbi[hi:hi + 1, :])
        cb_rows.append(jnp.concatenate([cbr, cbi], axis=1))
    cb = jnp.concatenate(cb_rows, axis=0)
    pstack = jnp.concatenate([p_re, -p_im], axis=0)
    kern = jnp.dot(cb, pstack, preferred_element_type=F32, precision=lax.Precision.HIGHEST)
    rr = lax.broadcasted_iota(jnp.int32, (nch * nch, blk), 0)
    ll = lax.broadcasted_iota(jnp.int32, (nch * nch, blk), 1)
    d_col = jnp.concatenate([d_ref[...]] * nch, axis=0)
    diag = ((rr >> 4) == (rr & (nch - 1))) & (ll == 0)
    k_sc[...] = kern + jnp.where(diag, d_col, 0.0)

    lag_s = blk_f - 1.0 - lag_l
    qmag = jnp.exp(rho_c * lag_s)
    q_re, q_im = qmag * jnp.cos(th_c * lag_s), qmag * jnp.sin(th_c * lag_s)
    for hi in range(nch):
        wr, wi = _cmul(q_re, q_im, bcr[:, hi:hi + 1], bci[:, hi:hi + 1])
        win_sc[:, hi * blk:(hi + 1) * blk] = jnp.concatenate([wr, wi], axis=0).astype(BF16)

    p1_re, p1_im = _cmul(p_re, p_im, abr_c, abi_c)
    ct_re, ct_im = ct_re_ref[...], ct_im_ref[...]
    for ho in range(nch):
        mr, mi = _cmul(p1_re, p1_im, ct_re[:, ho:ho + 1], ct_im[:, ho:ho + 1])
        m_sc[:, ho * blk:(ho + 1) * blk] = jnp.concatenate([mr, -mi], axis=0).astype(BF16)

    ucat = jnp.concatenate([u_ref[h] for h in range(nch)], axis=1)

    x = _dot_nt(ucat, win_sc[...])
    wmag = jnp.exp(rho_r * blk_f)
    w_re, w_im = wmag * jnp.cos(th_r * blk_f), wmag * jnp.sin(th_r * blk_f)
    row = lax.broadcasted_iota(jnp.int32, (nblk, 2 * half), 0)

    def shift_rows(a, k):
        return jnp.where(row >= k, pltpu.roll(a, k, 0), 0.0)

    def state_mul(a, wr, wi):
        return (a * jnp.concatenate([wr, wr], axis=1)
                + pltpu.roll(a, half, 1) * jnp.concatenate([-wi, wi], axis=1))

    step = 1
    while step < nblk:
        x = x + state_mul(shift_rows(x, step), w_re, w_im)
        w_re, w_im = _cmul(w_re, w_im, w_re, w_im)
        step *= 2
    acc_sc[...] = _dot(shift_rows(x, 1).astype(BF16), m_sc[...])

    srow = lax.broadcasted_iota(jnp.int32, (blk, blk), 0)
    lcol = lax.broadcasted_iota(jnp.int32, (blk, blk), 1)
    lower = lcol >= srow

    def build(pair, slab):
        for sub in range(2):
            for ho in range(nch):
                krow = k_sc[pl.ds((2 * pair + sub) * nch + ho, 1), :]
                tile = pltpu.roll(jnp.broadcast_to(krow, (blk, blk)), 0, 1, stride=1, stride_axis=0)
                slab[sub * blk:(sub + 1) * blk, ho * blk:(ho + 1) * blk] = (
                    jnp.where(lower, tile, 0.0).astype(BF16))

    def apply(pair, slab):
        u_pair = jnp.concatenate([u_ref[2 * pair], u_ref[2 * pair + 1]], axis=1)
        acc_sc[...] += _dot(u_pair, slab[...])

    npair = nch // 2
    build(0, ta_sc)

    def two_pairs(q, carry):
        build(2 * q + 1, tb_sc)
        apply(2 * q, ta_sc)
        build(jnp.minimum(2 * q + 2, npair - 1), ta_sc)
        apply(2 * q + 1, tb_sc)
        return carry

    lax.fori_loop(0, npair // 2, two_pairs, 0)
    for ho in range(nch):
        o_ref[ho] = acc_sc[:, ho * blk:(ho + 1) * blk]


def _s5_core(u_t, prm):
    s = u_t.shape[1]
    nblk = s // S5_BLOCK
    u4 = u_t.reshape(S5_GROUPS, S5_GROUP_CH, nblk, S5_BLOCK)

    def gspec(*dims):
        zeros = (0,) * len(dims)
        return pl.BlockSpec((None,) + dims, lambda g: (g,) + zeros)

    width = S5_GROUP_CH * S5_BLOCK
    out = pl.pallas_call(
        _s5_kernel,
        grid=(S5_GROUPS,),
        in_specs=[gspec(S5_GROUP_CH, nblk, S5_BLOCK),
                  gspec(S5_STATE, 1), gspec(S5_STATE, 1), gspec(1, S5_STATE), gspec(1, S5_STATE),
                  gspec(1, 1),
                  gspec(S5_STATE, S5_GROUP_CH), gspec(S5_STATE, S5_GROUP_CH),
                  gspec(S5_GROUP_CH, S5_STATE), gspec(S5_GROUP_CH, S5_STATE),
                  gspec(S5_GROUP_CH, S5_STATE), gspec(S5_GROUP_CH, S5_STATE),
                  gspec(S5_STATE, S5_GROUP_CH), gspec(S5_STATE, S5_GROUP_CH),
                  gspec(S5_GROUP_CH, 1)],
        out_specs=gspec(S5_GROUP_CH, nblk, S5_BLOCK),
        out_shape=jax.ShapeDtypeStruct(u4.shape, F32),
        scratch_shapes=[pltpu.VMEM((S5_GROUP_CH * S5_GROUP_CH, S5_BLOCK), F32),
                        pltpu.VMEM((2 * S5_BLOCK, width), BF16),
                        pltpu.VMEM((2 * S5_BLOCK, width), BF16),
                        pltpu.VMEM((nblk, width), F32),
                        pltpu.VMEM((2 * S5_STATE, width), BF16),
                        pltpu.VMEM((2 * S5_STATE, width), BF16)],
        compiler_params=pltpu.CompilerParams(dimension_semantics=("arbitrary",),
                                             vmem_limit_bytes=VMEM_LIMIT),
        name="s5_core",
    )(u4, *prm)
    return out.reshape(BRANCH, s)


def _s5_params(a_re, a_im, log_dt, b_re, b_im, c_re, c_im, d):
    g = S5_GROUPS
    return (a_re.reshape(g, S5_STATE, 1), a_im.reshape(g, S5_STATE, 1),
            a_re.reshape(g, 1, S5_STATE), a_im.reshape(g, 1, S5_STATE),
            log_dt.reshape(g, 1, 1),
            b_re, b_im,
            jnp.swapaxes(b_re, 1, 2), jnp.swapaxes(b_im, 1, 2),
            c_re, c_im,
            jnp.swapaxes(c_re, 1, 2), jnp.swapaxes(c_im, 1, 2),
            d.reshape(g, S5_GROUP_CH, 1))


_C_GQ, _C_GK, _C_GV, _C_GR = 0, 512, 1024, 1536
_C_MQ, _C_MK, _C_MV, _C_MO = 2048, 2560, 3072, 3584
_C_G0 = 4096
_C_LR = _C_G0 + 3 * D_MODEL
_C_IF = _C_LR + LANES
_W_COLS = _C_IF + LANES

GROUP = 4 * CHUNK
_LOG2_CHUNK = CHUNK.bit_length() - 1
_LOG2_HEAD_DIM = HEAD_DIM.bit_length() - 1
_GATE_SLICE = 256
_N_GATE_SLICES = 3 * D_MODEL // _GATE_SLICE


def _group_masks():
    r = lax.broadcasted_iota(jnp.int32, (GROUP, GROUP), 0)
    c = lax.broadcasted_iota(jnp.int32, (GROUP, GROUP), 1)
    causal = ((r >> _LOG2_CHUNK) == (c >> _LOG2_CHUNK)) & (r >= c)
    rr = lax.broadcasted_iota(jnp.int32, (4 * HEAD_DIM, GROUP), 0)
    cc = lax.broadcasted_iota(jnp.int32, (4 * HEAD_DIM, GROUP), 1)
    stacked = (rr >> _LOG2_HEAD_DIM) == (cc >> _LOG2_CHUNK)
    return causal, stacked


def _cumsum_chunks(tri, a):
    hi = a.astype(BF16)
    mid = (a - hi.astype(F32)).astype(BF16)
    return _dot(tri, hi) + _dot(tri, mid)


def _chunk_rows(c):
    return slice(c * CHUNK, (c + 1) * CHUNK)


def _per_chunk_sums(x_t, stacked, rhs):
    lhs = jnp.where(stacked, jnp.concatenate([x_t] * 4, axis=0), 0.0).astype(BF16)
    return _dot(lhs, rhs)


def _gla_group(q, k, v, la, states, causal, tri, stacked, filler):
    hd = HEAD_DIM
    b = _cumsum_chunks(tri, la)
    last = [b[c * CHUNK + CHUNK - 1:(c + 1) * CHUNK, :] for c in range(4)]
    b_last = jnp.concatenate([jnp.broadcast_to(r, (CHUNK, BRANCH)) for r in last], axis=0)
    q_dec = (q * (hd ** -0.5) * jnp.exp(b)).astype(BF16)
    k_inv = (k * jnp.exp(-b)).astype(BF16)
    k_end = (k * jnp.exp(b_last - b)).astype(BF16)
    vb = v.astype(BF16)
    decay = [jnp.exp(r) for r in last]
    outs, new_states = [], []
    for h in range(HEADS):
        sl = slice(h * hd, (h + 1) * hd)
        filler()
        scores = jnp.where(causal, _dot_nt(q_dec[:, sl], k_inv[:, sl]), 0.0).astype(BF16)
        intra = _dot(scores, vb[:, sl])
        kv = _per_chunk_sums(v[:, sl].T, stacked, k_end[:, sl])
        st = states[h]
        inter = []
        for c in range(4):
            inter.append(_dot_nt(q_dec[_chunk_rows(c), sl], st.astype(BF16)))
            st = st * decay[c][:, sl] + kv[c * hd:(c + 1) * hd, :]
        new_states.append(st)
        outs.append(intra + jnp.concatenate(inter, axis=0))
    return jnp.concatenate(outs, axis=1), new_states


def _mlstm_group(q, k, v, gates, states, m_run, causal, tri, stacked, filler):
    hd = HEAD_DIM
    cs = _cumsum_chunks(tri, gates)
    log_i = pltpu.roll(gates, HEADS, 1)
    w_rows, m_prev_rows, scale = [], [], []
    for c in range(4):
        rows = _chunk_rows(c)
        b_last = cs[c * CHUNK + CHUNK - 1:(c + 1) * CHUNK, :]
        g_end = b_last - cs[rows, :] + log_i[rows, :]
        m_new = jnp.maximum(b_last + m_run, jnp.max(g_end, axis=0, keepdims=True))
        w_rows.append(jnp.exp(g_end - m_new))
        scale.append(jnp.exp(b_last + m_run - m_new))
        m_prev_rows.append(jnp.broadcast_to(m_run, (CHUNK, LANES)))
        m_run = m_new
    w_all = jnp.concatenate(w_rows, axis=0)
    m_inter = cs + jnp.concatenate(m_prev_rows, axis=0)
    lane = lax.broadcasted_iota(jnp.int32, (GROUP, LANES), 1)
    z_t = jnp.where(lane < HEADS, gates, cs).T
    ones = jnp.ones((GROUP, hd), BF16)
    outs, new_states = [], []
    for h in range(HEADS):
        sl = slice(h * hd, (h + 1) * hd)
        col = HEADS + h
        filler()
        log_w = jnp.where(causal, cs[:, col:col + 1] - z_t[col:col + 1, :] + z_t[h:h + 1, :], -jnp.inf)
        m_int = m_inter[:, col:col + 1]
        m = jnp.maximum(m_int, jnp.max(log_w, axis=-1, keepdims=True))
        s_inter = jnp.exp(m_int - m)
        qb = q[:, sl].astype(BF16)
        kh = k[:, sl]
        v_aug = jnp.concatenate([v[:, sl].astype(BF16), ones], axis=1)
        qk = (_dot_nt(qb, kh.astype(BF16)) * jnp.exp(log_w - m)).astype(BF16)
        intra = _dot(qk, v_aug)
        kw_t = (kh * w_all[:, col:col + 1]).T
        kv = _per_chunk_sums(kw_t, stacked, v_aug)
        st = states[h]
        inter = []
        for c in range(4):
            inter.append(_dot(qb[_chunk_rows(c), :], st.astype(BF16)))
            st = scale[c][:, col:col + 1] * st + kv[c * hd:(c + 1) * hd, :]
        new_states.append(st)
        tot = s_inter * jnp.concatenate(inter, axis=0) + intra
        outs.append(tot[:, :hd] / jnp.maximum(jnp.abs(tot[:, hd:]), jnp.exp(-m)))
    return jnp.concatenate(outs, axis=1), new_states, m_run


def _mixer_kernel(x_ref, yst_ref, gpre_ref, gpost_ref, win_ref,
                  wgk_ref, bgk_ref, gnorm_ref, wgp_ref,
                  wglu_ref, bglu_ref, wsp_ref,
                  cw_ref, cb_ref, bif_ref, wmp_ref, wout_ref,
                  o_ref,
                  gst_sc, mqk_sc, mc_sc, mm_sc, carry_sc):
    tm = x_ref.shape[0]
    ngroup = tm // GROUP
    hd = HEAD_DIM

    @pl.when(pl.program_id(0) == 0)
    def _init():
        gst_sc[...] = jnp.zeros_like(gst_sc)
        mc_sc[...] = jnp.zeros_like(mc_sc)
        mm_sc[...] = jnp.zeros_like(mm_sc)
        carry_sc[...] = jnp.zeros_like(carry_sc)

    x = x_ref[...]
    u = _rms_norm(x, gpre_ref[...]).astype(BF16)

    def proj(c0, width):
        return _dot(u, win_ref[:, c0:c0 + width])

    causal, stacked = _group_masks()
    tri = causal.astype(BF16)

    done, queue = {}, []

    def filler(n=1):
        for _ in range(n):
            if queue:
                name, fn = queue.pop(0)
                done[name] = fn()

    def need(name):
        while name not in done:
            filler()
        return done[name]

    def s5_tail():
        ys = _gelu_tanh(yst_ref[...].T)
        glu = _sigmoid(_dot(ys.astype(BF16), wglu_ref[...]) + bglu_ref[...])
        return _dot((ys * glu).astype(BF16), wsp_ref[...])

    queue += [("mq", lambda: proj(_C_MQ, BRANCH)), ("mk", lambda: proj(_C_MK, BRANCH)),
              ("mv", lambda: proj(_C_MV, BRANCH)),
              ("pre_if", lambda: proj(_C_IF, LANES) + bif_ref[...]),
              ("mo", lambda: _sigmoid(proj(_C_MO, BRANCH))),
              ("g_r", lambda: _silu(proj(_C_GR, BRANCH)))]
    queue += [("gate%d" % i, functools.partial(
        lambda c0: _sigmoid(proj(c0, _GATE_SLICE)), _C_G0 + i * _GATE_SLICE))
        for i in range(_N_GATE_SLICES)]
    queue.insert(8, ("y_s5", s5_tail))
    fill_gla = -(-8 // (HEADS * ngroup))
    fill_ml = -(-(len(queue) + 1 - 8) // (HEADS * ngroup))

    gq, gk, gv = proj(_C_GQ, BRANCH), proj(_C_GK, BRANCH), proj(_C_GV, BRANCH)
    g_lr = proj(_C_LR, LANES).astype(BF16)
    log_a = _log_sigmoid(_dot(g_lr, wgk_ref[...]) + bgk_ref[...]) / GLA_GATE_NORM
    states = [gst_sc[h] for h in range(HEADS)]
    o_groups = []
    for g in range(ngroup):
        r = slice(g * GROUP, (g + 1) * GROUP)
        o_g, states = _gla_group(gq[r], gk[r], gv[r], log_a[r], states, causal, tri, stacked,
                                 functools.partial(filler, fill_gla))
        o_groups.append(o_g)
    for h in range(HEADS):
        gst_sc[h] = states[h]
    o = jnp.concatenate(o_groups, axis=0)
    gnorm = gnorm_ref[...]
    o_heads = [_rms_norm(o[:, h * hd:(h + 1) * hd], gnorm) for h in range(HEADS)]
    o_gated = (jnp.concatenate(o_heads, axis=1) * need("g_r")).astype(BF16)
    queue.insert(0, ("y_gla", lambda: _dot(o_gated, wgp_ref[...])))

    qk_pre = jnp.concatenate([need("mq"), need("mk")], axis=1)
    mqk_sc[0:SUBLANES, :] = carry_sc[...]
    mqk_sc[SUBLANES:, :] = qk_pre
    carry_sc[...] = qk_pre[tm - SUBLANES:, :]
    cw = cw_ref[...]
    conv = cb_ref[...] + cw[ML_CONV - 1:ML_CONV, :] * qk_pre
    for j in range(1, ML_CONV):
        conv = conv + cw[ML_CONV - 1 - j:ML_CONV - j, :] * mqk_sc[SUBLANES - j:SUBLANES - j + tm, :]
    qk_act = _silu(conv)
    mq = qk_act[:, 0:BRANCH]
    mk = qk_act[:, BRANCH:] * (hd ** -0.5)
    mv = need("mv")
    pre_if = need("pre_if")
    lane = lax.broadcasted_iota(jnp.int32, (tm, LANES), 1)
    gates = jnp.where(lane < HEADS, pre_if, _log_sigmoid(pre_if))
    states = [mc_sc[h] for h in range(HEADS)]
    m_run = mm_sc[0:1, :]
    h_groups = []
    for g in range(ngroup):
        r = slice(g * GROUP, (g + 1) * GROUP)
        h_g, states, m_run = _mlstm_group(mq[r], mk[r], mv[r], gates[r], states, m_run,
                                          causal, tri, stacked, functools.partial(filler, fill_ml))
        h_groups.append(h_g)
    for h in range(HEADS):
        mc_sc[h] = states[h]
    mm_sc[...] = jnp.broadcast_to(m_run, mm_sc.shape)
    hid = (need("mo") * jnp.concatenate(h_groups, axis=0)).astype(BF16)
    y_ml = _dot(hid, wmp_ref[...])

    per_branch = D_MODEL // _GATE_SLICE
    branch_gate = [jnp.concatenate([need("gate%d" % (b * per_branch + i)) for i in range(per_branch)],
                                   axis=1) for b in range(3)]
    merged = branch_gate[0] * need("y_gla") + branch_gate[1] * need("y_s5") + branch_gate[2] * y_ml
    mix = _dot(merged.astype(BF16), wout_ref[...])
    o_ref[...] = x + _rms_norm(mix, gpost_ref[...])


def _mixer(x, ys_t, w, tile):
    s = x.shape[0]
    weights = (w["gpre"], w["gpost"], w["win"], w["wgk"], w["bgk"], w["gnorm"], w["wgp"],
               w["wglu"], w["bglu"], w["wsp"], w["cw"], w["cb"], w["bif"], w["wmp"], w["wout"])
    return pl.pallas_call(
        _mixer_kernel,
        grid=(s // tile,),
        in_specs=[pl.BlockSpec((tile, D_MODEL), lambda i: (i, 0)),
                  pl.BlockSpec((BRANCH, tile), lambda i: (0, i))]
                 + [_full_spec(a.shape) for a in weights],
        out_specs=pl.BlockSpec((tile, D_MODEL), lambda i: (i, 0)),
        out_shape=jax.ShapeDtypeStruct((s, D_MODEL), F32),
        scratch_shapes=[pltpu.VMEM((HEADS, HEAD_DIM, HEAD_DIM), F32),
                        pltpu.VMEM((tile + SUBLANES, 2 * BRANCH), F32),
                        pltpu.VMEM((HEADS, HEAD_DIM, 2 * HEAD_DIM), F32),
                        pltpu.VMEM((SUBLANES, LANES), F32),
                        pltpu.VMEM((SUBLANES, 2 * BRANCH), F32)],
        compiler_params=pltpu.CompilerParams(dimension_semantics=("arbitrary",),
                                             vmem_limit_bytes=VMEM_LIMIT),
        name="mixer",
    )(x, ys_t, *weights)


_FFN_ROW_BLOCK = 256


def _ffn_kernel(x_ref, gpre_ref, gpost_ref, wup_ref, wgate_ref, cw_ref, cb_ref, wdown_ref,
                o_ref, a_sc, carry_sc):
    tm = x_ref.shape[0]
    rb = min(_FFN_ROW_BLOCK, tm)

    @pl.when(pl.program_id(0) == 0)
    def _zero_carry():
        carry_sc[...] = jnp.zeros_like(carry_sc)

    a_sc[0:SUBLANES, :] = carry_sc[...]
    cw = cw_ref[...]
    for r in range(tm // rb):
        rows = slice(r * rb, (r + 1) * rb)
        x = x_ref[rows, :]
        u = _rms_norm(x, gpre_ref[...]).astype(BF16)
        a = _dot(u, wup_ref[...])
        gate = _dot(u, wgate_ref[...])
        a_sc[SUBLANES + r * rb:SUBLANES + (r + 1) * rb, :] = a
        conv = cb_ref[...] + cw[FFN_CONV - 1:FFN_CONV, :] * a
        for j in range(1, FFN_CONV):
            lo = SUBLANES + r * rb - j
            conv = conv + cw[FFN_CONV - 1 - j:FFN_CONV - j, :] * a_sc[lo:lo + rb, :]
        h = (_gelu_tanh(conv) * gate).astype(BF16)
        o_ref[rows, :] = x + _rms_norm(_dot(h, wdown_ref[...]), gpost_ref[...])

    carry_sc[...] = a_sc[tm:tm + SUBLANES, :]


def _ffn(x, w, tile):
    s = x.shape[0]
    weights = (w["gpre"], w["gpost"], w["wup"], w["wgate"], w["cw"], w["cb"], w["wdown"])
    return pl.pallas_call(
        _ffn_kernel,
        grid=(s // tile,),
        in_specs=[pl.BlockSpec((tile, D_MODEL), lambda i: (i, 0))]
                 + [_full_spec(a.shape) for a in weights],
        out_specs=pl.BlockSpec((tile, D_MODEL), lambda i: (i, 0)),
        out_shape=jax.ShapeDtypeStruct((s, D_MODEL), F32),
        scratch_shapes=[pltpu.VMEM((tile + SUBLANES, D_FF), F32),
                        pltpu.VMEM((SUBLANES, D_FF), F32)],
        compiler_params=pltpu.CompilerParams(dimension_semantics=("arbitrary",),
                                             vmem_limit_bytes=VMEM_LIMIT),
        name="conv_ffn",
    )(x, *weights)


def _pad_cols(a, width):
    return jnp.pad(a, ((0, 0), (0, width - a.shape[1])))


def _pack_layer(l, p):
    w_in = p["w_in"][l].astype(BF16)
    g_qkv, g_lr, g_r = w_in[:, 0:1536], w_in[:, 1536:1552], w_in[:, 1552:2064]
    s_u, m_qkv, m_if, m_o_gates = (w_in[:, 2064:2576], w_in[:, 2576:4112], w_in[:, 4112:4120],
                                   w_in[:, 4120:7704])
    win = jnp.concatenate(
        [g_qkv, g_r, m_qkv, m_o_gates, _pad_cols(g_lr, LANES), _pad_cols(m_if, LANES)], axis=1)
    row = lambda a: a.reshape(1, -1)
    mixer = dict(
        gpre=row(p["norm_mix_pre"][l]), gpost=row(p["norm_mix_post"][l]), win=win,
        wgk=jnp.pad(p["gla_w_gk"][l], ((0, LANES - GLA_RANK), (0, 0))).astype(BF16),
        bgk=row(p["gla_b_gk"][l]), gnorm=row(p["gla_norm"][l]),
        wgp=p["gla_w_proj"][l].astype(BF16),
        wglu=p["s5_w_glu"][l].astype(BF16), bglu=row(p["s5_b_glu"][l]),
        wsp=p["s5_w_proj"][l].astype(BF16),
        cw=p["ml_conv_w"][l], cb=row(p["ml_conv_b"][l]),
        bif=_pad_cols(row(jnp.concatenate([p["ml_b_i"][l], p["ml_b_f"][l]])), LANES),
        wmp=p["ml_w_proj"][l].astype(BF16), wout=p["w_out"][l].astype(BF16))
    s5_w = s_u.T
    s5_prm = _s5_params(p["s5_a_re"][l], p["s5_a_im"][l], p["s5_log_dt"][l], p["s5_b_re"][l],
                        p["s5_b_im"][l], p["s5_c_re"][l], p["s5_c_im"][l], p["s5_d"][l])
    ffn = dict(gpre=row(p["norm_ffn_pre"][l]), gpost=row(p["norm_ffn_post"][l]),
               wup=p["ffn_w_up"][l].astype(BF16), wgate=p["ffn_w_gate"][l].astype(BF16),
               cw=p["ffn_conv_w"][l], cb=row(p["ffn_conv_b"][l]),
               wdown=p["ffn_w_down"][l].astype(BF16))
    return mixer, s5_w, s5_prm, ffn


def _tiles(s):
    return min(1024, s), min(512, s), min(512, s)


def kernel(x, norm_mix_pre, norm_mix_post, norm_ffn_pre, norm_ffn_post, w_in, gla_w_gk, gla_b_gk, gla_norm, gla_w_proj, s5_a_re, s5_a_im, s5_log_dt, s5_b_re, s5_b_im, s5_c_re, s5_c_im, s5_d, s5_w_glu, s5_b_glu, s5_w_proj, ml_conv_w, ml_conv_b, ml_b_i, ml_b_f, ml_w_proj, w_out, ffn_w_up, ffn_w_gate, ffn_conv_w, ffn_conv_b, ffn_w_down):
    p = dict(norm_mix_pre=norm_mix_pre, norm_mix_post=norm_mix_post, norm_ffn_pre=norm_ffn_pre,
             norm_ffn_post=norm_ffn_post, w_in=w_in, gla_w_gk=gla_w_gk, gla_b_gk=gla_b_gk,
             gla_norm=gla_norm, gla_w_proj=gla_w_proj, s5_a_re=s5_a_re, s5_a_im=s5_a_im,
             s5_log_dt=s5_log_dt, s5_b_re=s5_b_re, s5_b_im=s5_b_im, s5_c_re=s5_c_re,
             s5_c_im=s5_c_im, s5_d=s5_d, s5_w_glu=s5_w_glu, s5_b_glu=s5_b_glu, s5_w_proj=s5_w_proj,
             ml_conv_w=ml_conv_w, ml_conv_b=ml_conv_b, ml_b_i=ml_b_i, ml_b_f=ml_b_f,
             ml_w_proj=ml_w_proj, w_out=w_out, ffn_w_up=ffn_w_up, ffn_w_gate=ffn_w_gate,
             ffn_conv_w=ffn_conv_w, ffn_conv_b=ffn_conv_b, ffn_w_down=ffn_w_down)
    bsz, s, _ = x.shape
    assert bsz == 1 and s % (S5_BLOCK * SUBLANES) == 0
    t_s5, t_mix, t_ffn = _tiles(s)
    h = x[0]
    for l in range(w_in.shape[0]):
        mixer_w, s5_w, s5_prm, ffn_w = _pack_layer(l, p)
        u_t = _s5_in_proj(h, mixer_w["gpre"], s5_w, t_s5)
        y_t = _s5_core(u_t, s5_prm)
        h = _mixer(h, y_t, mixer_w, t_mix)
        h = _ffn(h, ffn_w, t_ffn)
    return h[None]
```

```python
import functools
import math

import jax
import jax.numpy as jnp
from jax import lax
from jax.experimental import pallas as pl
from jax.experimental.pallas import tpu as pltpu

F32 = jnp.float32
BF16 = jnp.bfloat16

D_MODEL = 1024
CHUNK = 64
NORM_EPS = 1e-6
BRANCH = 512
HEADS = 4
HEAD_DIM = BRANCH // HEADS
GLA_RANK = 16
GLA_GATE_NORM = 16.0
S5_GROUP_CH = 16
S5_GROUPS = BRANCH // S5_GROUP_CH
S5_STATE = 64
S5_BLOCK = 128
ML_CONV = 4
D_FF = 2816
FFN_CONV = 3
LANES = 128
SUBLANES = 8

VMEM_LIMIT = 56 * 1024 * 1024


def _rms_norm(x, gain):
    return x * lax.rsqrt(jnp.mean(x * x, axis=-1, keepdims=True) + NORM_EPS) * gain


def _dot(a, b):
    return jnp.dot(a, b, preferred_element_type=F32)


def _dot_nt(a, b):
    return lax.dot_general(a, b, (((1,), (1,)), ((), ())), preferred_element_type=F32)


def _log_sigmoid(x):
    return jnp.minimum(x, 0.0) - jnp.log(1.0 + jnp.exp(-jnp.abs(x)))


def _sigmoid(x):
    return 1.0 / (1.0 + jnp.exp(-x))


def _silu(x):
    return x * _sigmoid(x)


def _gelu_tanh(x):
    c = math.sqrt(2.0 / math.pi)
    return x * (0.5 * (1.0 + jnp.tanh(c * (x + 0.044715 * (x * x * x)))))


def _causal_conv_rows(x, prev, w, bias):
    k = w.shape[0]
    row = lax.broadcasted_iota(jnp.int32, (SUBLANES, x.shape[1]), 0)
    acc = w[0:1, :] * x
    for step in range(1, k):
        first = w[0:1, :] * prev[SUBLANES - step:SUBLANES - step + 1, :]
        for j in range(1, step):
            first = first + w[j:j + 1, :] * prev[SUBLANES - step + j:SUBLANES - step + j + 1, :]
        rolled = pltpu.roll(acc, 1, 0)
        head = jnp.where(row == 0, first, rolled[0:SUBLANES, :])
        acc = jnp.concatenate([head, rolled[SUBLANES:, :]], axis=0) + w[step:step + 1, :] * x
    return acc + bias


def _full_spec(shape):
    zeros = (0,) * len(shape)
    return pl.BlockSpec(shape, lambda *_: zeros, pipeline_mode=pl.Buffered(1))


def _s5_in_kernel(x_ref, gain_ref, w_ref, o_ref):
    u = _rms_norm(x_ref[...], gain_ref[...]).astype(BF16)
    o_ref[...] = _dot_nt(w_ref[...], u).astype(BF16)


def _s5_in_proj(x, gain, w_t, tile):
    s = x.shape[0]
    return pl.pallas_call(
        _s5_in_kernel,
        grid=(s // tile,),
        in_specs=[pl.BlockSpec((tile, D_MODEL), lambda i: (i, 0)),
                  _full_spec((1, D_MODEL)),
                  _full_spec((BRANCH, D_MODEL))],
        out_specs=pl.BlockSpec((BRANCH, tile), lambda i: (0, i)),
        out_shape=jax.ShapeDtypeStruct((BRANCH, s), BF16),
        compiler_params=pltpu.CompilerParams(dimension_semantics=("arbitrary",),
                                             vmem_limit_bytes=VMEM_LIMIT),
        name="s5_in_proj",
    )(x, gain, w_t)


def _cmul(xr, xi, yr, yi):
    return xr * yr - xi * yi, xr * yi + xi * yr


def _s5_kernel(u_ref, are_c_ref, aim_c_ref, are_r_ref, aim_r_ref, ldt_ref,
               b_re_ref, b_im_ref, bt_re_ref, bt_im_ref, c_re_ref, c_im_ref, ct_re_ref, ct_im_ref, d_ref,
               o_ref, k_sc, ta_sc, tb_sc, acc_sc, win_sc, m_sc):
    nch, nblk, blk = u_ref.shape
    half = S5_STATE
    blk_f = float(blk)
    dt = jnp.exp(ldt_ref[...])

    def zoh_factor(are, aim, abr, abi):
        den = are * are + aim * aim
        zr = abr - 1.0
        return (zr * are + abi * aim) / den, (abi * are - zr * aim) / den

    are_r, aim_r = are_r_ref[...], aim_r_ref[...]
    rho_r, th_r = dt * are_r, dt * aim_r
    mag_r = jnp.exp(rho_r)
    fr_r, fi_r = zoh_factor(are_r, aim_r, mag_r * jnp.cos(th_r), mag_r * jnp.sin(th_r))
    bbr, bbi = _cmul(fr_r, fi_r, bt_re_ref[...], bt_im_ref[...])

    are_c, aim_c = are_c_ref[...], aim_c_ref[...]
    rho_c, th_c = dt * are_c, dt * aim_c
    lag_l = lax.broadcasted_iota(jnp.int32, (half, blk), 1).astype(F32)
    pmag = jnp.exp(rho_c * lag_l)
    p_re, p_im = pmag * jnp.cos(th_c * lag_l), pmag * jnp.sin(th_c * lag_l)
    abr_c, abi_c = p_re[:, 1:2], p_im[:, 1:2]
    fr_c, fi_c = zoh_factor(are_c, aim_c, abr_c, abi_c)
    bcr, bci = _cmul(fr_c, fi_c, b_re_ref[...], b_im_ref[...])

    c_re, c_im = c_re_ref[...], c_im_ref[...]
    cb_rows = []
    for hi in range(nch):
        cbr, cbi = _cmul(c_re, c_im, bbr[hi:hi + 1, :], bbi[hi:hi + 1, :])
        cb_rows.append(jnp.concatenate([cbr, cbi], axis=1))
    cb = jnp.concatenate(cb_rows, axis=0)
    pstack = jnp.concatenate([p_re, -p_im], axis=0)
    kern = jnp.dot(cb, pstack, preferred_element_type=F32, precision=lax.Precision.HIGHEST)
    rr = lax.broadcasted_iota(jnp.int32, (nch * nch, blk), 0)
    ll = lax.broadcasted_iota(jnp.int32, (nch * nch, blk), 1)
    d_col = jnp.concatenate([d_ref[...]] * nch, axis=0)
    diag = ((rr >> 4) == (rr & (nch - 1))) & (ll == 0)
    k_sc[...] = kern + jnp.where(diag, d_col, 0.0)

    lag_s = blk_f - 1.0 - lag_l
    qmag = jnp.exp(rho_c * lag_s)
    q_re, q_im = qmag * jnp.cos(th_c * lag_s), qmag * jnp.sin(th_c * lag_s)
    for hi in range(nch):
        wr, wi = _cmul(q_re, q_im, bcr[:, hi:hi + 1], bci[:, hi:hi + 1])
        win_sc[:, hi * blk:(hi + 1) * blk] = jnp.concatenate([wr, wi], axis=0).astype(BF16)

    p1_re, p1_im = _cmul(p_re, p_im, abr_c, abi_c)
    ct_re, ct_im = ct_re_ref[...], ct_im_ref[...]
    for ho in range(nch):
        mr, mi = _cmul(p1_re, p1_im, ct_re[:, ho:ho + 1], ct_im[:, ho:ho + 1])
        m_sc[:, ho * blk:(ho + 1) * blk] = jnp.concatenate([mr, -mi], axis=0).astype(BF16)

    ucat = jnp.concatenate([u_ref[h] for h in range(nch)], axis=1)

    x = _dot_nt(ucat, win_sc[...])
    wmag = jnp.exp(rho_r * blk_f)
    w_re, w_im = wmag * jnp.cos(th_r * blk_f), wmag * jnp.sin(th_r * blk_f)
    row = lax.broadcasted_iota(jnp.int32, (nblk, 2 * half), 0)

    def shift_rows(a, k):
        return jnp.where(row >= k, pltpu.roll(a, k, 0), 0.0)

    def state_mul(a, wr, wi):
        return (a * jnp.concatenate([wr, wr], axis=1)
                + pltpu.roll(a, half, 1) * jnp.concatenate([-wi, wi], axis=1))

    step = 1
    while step < nblk:
        x = x + state_mul(shift_rows(x, step), w_re, w_im)
        w_re, w_im = _cmul(w_re, w_im, w_re, w_im)
        step *= 2
    acc_sc[...] = _dot(shift_rows(x, 1).astype(BF16), m_sc[...])

    srow = lax.broadcasted_iota(jnp.int32, (blk, blk), 0)
    lcol = lax.broadcasted_iota(jnp.int32, (blk, blk), 1)
    lower = lcol >= srow

    def build(pair, slab):
        for sub in range(2):
            for ho in range(nch):
                krow = k_sc[pl.ds((2 * pair + sub) * nch + ho, 1), :]
                tile = pltpu.roll(jnp.broadcast_to(krow, (blk, blk)), 0, 1, stride=1, stride_axis=0)
                slab[sub * blk:(sub + 1) * blk, ho * blk:(ho + 1) * blk] = (
                    jnp.where(lower, tile, 0.0).astype(BF16))

    def apply(pair, slab):
        u_pair = jnp.concatenate([u_ref[2 * pair], u_ref[2 * pair + 1]], axis=1)
        acc_sc[...] += _dot(u_pair, slab[...])

    npair = nch // 2
    build(0, ta_sc)

    def two_pairs(q, carry):
        build(2 * q + 1, tb_sc)
        apply(2 * q, ta_sc)
        build(jnp.minimum(2 * q + 2, npair - 1), ta_sc)
        apply(2 * q + 1, tb_sc)
        return carry

    lax.fori_loop(0, npair // 2, two_pairs, 0)
    for ho in range(nch):
        o_ref[ho] = acc_sc[:, ho * blk:(ho + 1) * blk]


def _s5_core(u_t, prm):
    s = u_t.shape[1]
    nblk = s // S5_BLOCK
    u4 = u_t.reshape(S5_GROUPS, S5_GROUP_CH, nblk, S5_BLOCK)

    def gspec(*dims):
        zeros = (0,) * len(dims)
        return pl.BlockSpec((None,) + dims, lambda g: (g,) + zeros)

    width = S5_GROUP_CH * S5_BLOCK
    out = pl.pallas_call(
        _s5_kernel,
        grid=(S5_GROUPS,),
        in_specs=[gspec(S5_GROUP_CH, nblk, S5_BLOCK),
                  gspec(S5_STATE, 1), gspec(S5_STATE, 1), gspec(1, S5_STATE), gspec(1, S5_STATE),
                  gspec(1, 1),
                  gspec(S5_STATE, S5_GROUP_CH), gspec(S5_STATE, S5_GROUP_CH),
                  gspec(S5_GROUP_CH, S5_STATE), gspec(S5_GROUP_CH, S5_STATE),
                  gspec(S5_GROUP_CH, S5_STATE), gspec(S5_GROUP_CH, S5_STATE),
                  gspec(S5_STATE, S5_GROUP_CH), gspec(S5_STATE, S5_GROUP_CH),
                  gspec(S5_GROUP_CH, 1)],
        out_specs=gspec(S5_GROUP_CH, nblk, S5_BLOCK),
        out_shape=jax.ShapeDtypeStruct(u4.shape, F32),
        scratch_shapes=[pltpu.VMEM((S5_GROUP_CH * S5_GROUP_CH, S5_BLOCK), F32),
                        pltpu.VMEM((2 * S5_BLOCK, width), BF16),
                        pltpu.VMEM((2 * S5_BLOCK, width), BF16),
                        pltpu.VMEM((nblk, width), F32),
                        pltpu.VMEM((2 * S5_STATE, width), BF16),
                        pltpu.VMEM((2 * S5_STATE, width), BF16)],
        compiler_params=pltpu.CompilerParams(dimension_semantics=("arbitrary",),
                                             vmem_limit_bytes=VMEM_LIMIT),
        name="s5_core",
    )(u4, *prm)
    return out.reshape(BRANCH, s)


def _s5_params(a_re, a_im, log_dt, b_re, b_im, c_re, c_im, d):
    g = S5_GROUPS
    return (a_re.reshape(g, S5_STATE, 1), a_im.reshape(g, S5_STATE, 1),
            a_re.reshape(g, 1, S5_STATE), a_im.reshape(g, 1, S5_STATE),
            log_dt.reshape(g, 1, 1),
            b_re, b_im,
            jnp.swapaxes(b_re, 1, 2), jnp.swapaxes(b_im, 1, 2),
            c_re, c_im,
            jnp.swapaxes(c_re, 1, 2), jnp.swapaxes(c_im, 1, 2),
            d.reshape(g, S5_GROUP_CH, 1))


_C_GQ, _C_GK, _C_GV, _C_GR = 0, 512, 1024, 1536
_C_MQ, _C_MK, _C_MV, _C_MO = 2048, 2560, 3072, 3584
_C_G0 = 4096
_C_LR = _C_G0 + 3 * D_MODEL
_C_IF = _C_LR + LANES
_W_COLS = _C_IF + LANES

GROUP = 4 * CHUNK
_LOG2_CHUNK = CHUNK.bit_length() - 1
_LOG2_HEAD_DIM = HEAD_DIM.bit_length() - 1
_GATE_SLICE = 256
_N_GATE_SLICES = 3 * D_MODEL // _GATE_SLICE


def _group_masks():
    r = lax.broadcasted_iota(jnp.int32, (GROUP, GROUP), 0)
    c = lax.broadcasted_iota(jnp.int32, (GROUP, GROUP), 1)
    causal = ((r >> _LOG2_CHUNK) == (c >> _LOG2_CHUNK)) & (r >= c)
    rr = lax.broadcasted_iota(jnp.int32, (4 * HEAD_DIM, GROUP), 0)
    cc = lax.broadcasted_iota(jnp.int32, (4 * HEAD_DIM, GROUP), 1)
    stacked = (rr >> _LOG2_HEAD_DIM) == (cc >> _LOG2_CHUNK)
    return causal, stacked


def _cumsum_chunks(tri, a):
    hi = a.astype(BF16)
    mid = (a - hi.astype(F32)).astype(BF16)
    return _dot(tri, hi) + _dot(tri, mid)


def _chunk_rows(c):
    return slice(c * CHUNK, (c + 1) * CHUNK)


def _per_chunk_sums(x_t, stacked, rhs):
    lhs = jnp.where(stacked, jnp.concatenate([x_t] * 4, axis=0), 0.0).astype(BF16)
    return _dot(lhs, rhs)


def _gla_group(q, k, v, la, states, causal, tri, stacked, filler):
    hd = HEAD_DIM
    b = _cumsum_chunks(tri, la)
    last = [b[c * CHUNK + CHUNK - 1:(c + 1) * CHUNK, :] for c in range(4)]
    b_last = jnp.concatenate([jnp.broadcast_to(r, (CHUNK, BRANCH)) for r in last], axis=0)
    q_dec = (q * (hd ** -0.5) * jnp.exp(b)).astype(BF16)
    k_inv = (k * jnp.exp(-b)).astype(BF16)
    k_end = (k * jnp.exp(b_last - b)).astype(BF16)
    vb = v.astype(BF16)
    decay = [jnp.exp(r) for r in last]
    outs, new_states = [], []
    for h in range(HEADS):
        sl = slice(h * hd, (h + 1) * hd)
        filler()
        scores = jnp.where(causal, _dot_nt(q_dec[:, sl], k_inv[:, sl]), 0.0).astype(BF16)
        intra = _dot(scores, vb[:, sl])
        kv = _per_chunk_sums(v[:, sl].T, stacked, k_end[:, sl])
        st = states[h]
        inter = []
        for c in range(4):
            inter.append(_dot_nt(q_dec[_chunk_rows(c), sl], st.astype(BF16)))
            st = st * decay[c][:, sl] + kv[c * hd:(c + 1) * hd, :]
        new_states.append(st)
        outs.append(intra + jnp.concatenate(inter, axis=0))
    return jnp.concatenate(outs, axis=1), new_states


def _mlstm_group(q, k, v, gates, states, m_run, causal, tri, stacked, filler):
    hd = HEAD_DIM
    cs = _cumsum_chunks(tri, gates)
    log_i = pltpu.roll(gates, HEADS, 1)
    w_rows, m_prev_rows, scale = [], [], []
    for c in range(4):
        rows = _chunk_rows(c)
        b_last = cs[c * CHUNK + CHUNK - 1:(c + 1) * CHUNK, :]
        g_end = b_last - cs[rows, :] + log_i[rows, :]
        m_new = jnp.maximum(b_last + m_run, jnp.max(g_end, axis=0, keepdims=True))
        w_rows.append(jnp.exp(g_end - m_new))
        scale.append(jnp.exp(b_last + m_run - m_new))
        m_prev_rows.append(jnp.broadcast_to(m_run, (CHUNK, LANES)))
        m_run = m_new
    w_all = jnp.concatenate(w_rows, axis=0)
    m_inter = cs + jnp.concatenate(m_prev_rows, axis=0)
    lane = lax.broadcasted_iota(jnp.int32, (GROUP, LANES), 1)
    z_t = jnp.where(lane < HEADS, gates, cs).T
    ones = jnp.ones((GROUP, hd), BF16)
    outs, new_states = [], []
    for h in range(HEADS):
        sl = slice(h * hd, (h + 1) * hd)
        col = HEADS + h
        filler()
        log_w = jnp.where(causal, cs[:, col:col + 1] - z_t[col:col + 1, :] + z_t[h:h + 1, :], -jnp.inf)
        m_int = m_inter[:, col:col + 1]
        m = jnp.maximum(m_int, jnp.max(log_w, axis=-1, keepdims=True))
        s_inter = jnp.exp(m_int - m)
        qb = q[:, sl].astype(BF16)
        kh = k[:, sl]
        v_aug = jnp.concatenate([v[:, sl].astype(BF16), ones], axis=1)
        qk = (_dot_nt(qb, kh.astype(BF16)) * jnp.exp(log_w - m)).astype(BF16)
        intra = _dot(qk, v_aug)
        kw_t = (kh * w_all[:, col:col + 1]).T
        kv = _per_chunk_sums(kw_t, stacked, v_aug)
        st = states[h]
        inter = []
        for c in range(4):
            inter.append(_dot(qb[_chunk_rows(c), :], st.astype(BF16)))
            st = scale[c][:, col:col + 1] * st + kv[c * hd:(c + 1) * hd, :]
        new_states.append(st)
        tot = s_inter * jnp.concatenate(inter, axis=0) + intra
        outs.append(tot[:, :hd] / jnp.maximum(jnp.abs(tot[:, hd:]), jnp.exp(-m)))
    return jnp.concatenate(outs, axis=1), new_states, m_run


def _mixer_kernel(x_ref, yst_ref, gpre_ref, gpost_ref, win_ref,
                  wgk_ref, bgk_ref, gnorm_ref, wgp_ref,
                  wglu_ref, bglu_ref, wsp_ref,
                  cw_ref, cb_ref, bif_ref, wmp_ref, wout_ref,
                  o_ref,
                  gst_sc, mc_sc, mm_sc, carry_sc):
    tm = x_ref.shape[0]
    ngroup = tm // GROUP
    hd = HEAD_DIM

    @pl.when(pl.program_id(0) == 0)
    def _init():
        gst_sc[...] = jnp.zeros_like(gst_sc)
        mc_sc[...] = jnp.zeros_like(mc_sc)
        mm_sc[...] = jnp.zeros_like(mm_sc)
        carry_sc[...] = jnp.zeros_like(carry_sc)

    x = x_ref[...]
    u = _rms_norm(x, gpre_ref[...]).astype(BF16)

    def proj(c0, width):
        return _dot(u, win_ref[:, c0:c0 + width])

    causal, stacked = _group_masks()
    tri = causal.astype(BF16)

    done, queue = {}, []

    def filler(n=1):
        for _ in range(n):
            if queue:
                name, fn = queue.pop(0)
                done[name] = fn()

    def need(name):
        while name not in done:
            filler()
        return done[name]

    def s5_tail():
        ys = _gelu_tanh(yst_ref[...].T)
        glu = _sigmoid(_dot(ys.astype(BF16), wglu_ref[...]) + bglu_ref[...])
        return _dot((ys * glu).astype(BF16), wsp_ref[...])

    queue += [("mq", lambda: proj(_C_MQ, BRANCH)), ("mk", lambda: proj(_C_MK, BRANCH)),
              ("mv", lambda: proj(_C_MV, BRANCH)),
              ("pre_if", lambda: proj(_C_IF, LANES) + bif_ref[...]),
              ("mo", lambda: _sigmoid(proj(_C_MO, BRANCH))),
              ("g_r", lambda: _silu(proj(_C_GR, BRANCH)))]
    queue += [("gate%d" % i, functools.partial(
        lambda c0: _sigmoid(proj(c0, _GATE_SLICE)), _C_G0 + i * _GATE_SLICE))
        for i in range(_N_GATE_SLICES)]
    queue.insert(8, ("y_s5", s5_tail))
    fill_gla = -(-8 // (HEADS * ngroup))
    fill_ml = -(-(len(queue) + 1 - 8) // (HEADS * ngroup))

    gq, gk, gv = proj(_C_GQ, BRANCH), proj(_C_GK, BRANCH), proj(_C_GV, BRANCH)
    g_lr = proj(_C_LR, LANES).astype(BF16)
    log_a = _log_sigmoid(_dot(g_lr, wgk_ref[...]) + bgk_ref[...]) / GLA_GATE_NORM
    states = [gst_sc[h] for h in range(HEADS)]
    o_groups = []
    for g in range(ngroup):
        r = slice(g * GROUP, (g + 1) * GROUP)
        o_g, states = _gla_group(gq[r], gk[r], gv[r], log_a[r], states, causal, tri, stacked,
                                 functools.partial(filler, fill_gla))
        o_groups.append(o_g)
    for h in range(HEADS):
        gst_sc[h] = states[h]
    o = jnp.concatenate(o_groups, axis=0)
    gnorm = gnorm_ref[...]
    o_heads = [_rms_norm(o[:, h * hd:(h + 1) * hd], gnorm) for h in range(HEADS)]
    o_gated = (jnp.concatenate(o_heads, axis=1) * need("g_r")).astype(BF16)
    queue.insert(0, ("y_gla", lambda: _dot(o_gated, wgp_ref[...])))

    qk_pre = jnp.concatenate([need("mq"), need("mk")], axis=1)
    prev = carry_sc[...]
    carry_sc[...] = qk_pre[tm - SUBLANES:, :]
    conv = _causal_conv_rows(qk_pre, prev, cw_ref[...], cb_ref[...])
    qk_act = _silu(conv)
    mq = qk_act[:, 0:BRANCH]
    mk = qk_act[:, BRANCH:] * (hd ** -0.5)
    mv = need("mv")
    pre_if = need("pre_if")
    lane = lax.broadcasted_iota(jnp.int32, (tm, LANES), 1)
    gates = jnp.where(lane < HEADS, pre_if, _log_sigmoid(pre_if))
    states = [mc_sc[h] for h in range(HEADS)]
    m_run = mm_sc[0:1, :]
    h_groups = []
    for g in range(ngroup):
        r = slice(g * GROUP, (g + 1) * GROUP)
        h_g, states, m_run = _mlstm_group(mq[r], mk[r], mv[r], gates[r], states, m_run,
                                          causal, tri, stacked, functools.partial(filler, fill_ml))
        h_groups.append(h_g)
    for h in range(HEADS):
        mc_sc[h] = states[h]
    mm_sc[...] = jnp.broadcast_to(m_run, mm_sc.shape)
    hid = (need("mo") * jnp.concatenate(h_groups, axis=0)).astype(BF16)
    y_ml = _dot(hid, wmp_ref[...])

    per_branch = D_MODEL // _GATE_SLICE
    branch_gate = [jnp.concatenate([need("gate%d" % (b * per_branch + i)) for i in range(per_branch)],
                                   axis=1) for b in range(3)]
    merged = branch_gate[0] * need("y_gla") + branch_gate[1] * need("y_s5") + branch_gate[2] * y_ml
    mix = _dot(merged.astype(BF16), wout_ref[...])
    o_ref[...] = x + _rms_norm(mix, gpost_ref[...])


def _mixer(x, ys_t, w, tile):
    s = x.shape[0]
    weights = (w["gpre"], w["gpost"], w["win"], w["wgk"], w["bgk"], w["gnorm"], w["wgp"],
               w["wglu"], w["bglu"], w["wsp"], w["cw"], w["cb"], w["bif"], w["wmp"], w["wout"])
    return pl.pallas_call(
        _mixer_kernel,
        grid=(s // tile,),
        in_specs=[pl.BlockSpec((tile, D_MODEL), lambda i: (i, 0)),
                  pl.BlockSpec((BRANCH, tile), lambda i: (0, i))]
                 + [_full_spec(a.shape) for a in weights],
        out_specs=pl.BlockSpec((tile, D_MODEL), lambda i: (i, 0)),
        out_shape=jax.ShapeDtypeStruct((s, D_MODEL), F32),
        scratch_shapes=[pltpu.VMEM((HEADS, HEAD_DIM, HEAD_DIM), F32),
                        pltpu.VMEM((HEADS, HEAD_DIM, 2 * HEAD_DIM), F32),
                        pltpu.VMEM((SUBLANES, LANES), F32),
                        pltpu.VMEM((SUBLANES, 2 * BRANCH), F32)],
        compiler_params=pltpu.CompilerParams(dimension_semantics=("arbitrary",),
                                             vmem_limit_bytes=VMEM_LIMIT),
        name="mixer",
    )(x, ys_t, *weights)


_FFN_ROW_BLOCK = 256


def _ffn_kernel(x_ref, gpre_ref, gpost_ref, wup_ref, wgate_ref, cw_ref, cb_ref, wdown_ref,
                o_ref, a_sc, carry_sc):
    tm = x_ref.shape[0]
    rb = min(_FFN_ROW_BLOCK, tm)

    @pl.when(pl.program_id(0) == 0)
    def _zero_carry():
        carry_sc[...] = jnp.zeros_like(carry_sc)

    a_sc[0:SUBLANES, :] = carry_sc[...]
    cw = cw_ref[...]
    for r in range(tm // rb):
        rows = slice(r * rb, (r + 1) * rb)
        x = x_ref[rows, :]
        u = _rms_norm(x, gpre_ref[...]).astype(BF16)
        a = _dot(u, wup_ref[...])
        gate = _dot(u, wgate_ref[...])
        a_sc[SUBLANES + r * rb:SUBLANES + (r + 1) * rb, :] = a
        conv = cb_ref[...] + cw[FFN_CONV - 1:FFN_CONV, :] * a
        for j in range(1, FFN_CONV):
            lo = SUBLANES + r * rb - j
            conv = conv + cw[FFN_CONV - 1 - j:FFN_CONV - j, :] * a_sc[lo:lo + rb, :]
        h = (_gelu_tanh(conv) * gate).astype(BF16)
        o_ref[rows, :] = x + _rms_norm(_dot(h, wdown_ref[...]), gpost_ref[...])

    carry_sc[...] = a_sc[tm:tm + SUBLANES, :]


def _ffn(x, w, tile):
    s = x.shape[0]
    weights = (w["gpre"], w["gpost"], w["wup"], w["wgate"], w["cw"], w["cb"], w["wdown"])
    return pl.pallas_call(
        _ffn_kernel,
        grid=(s // tile,),
        in_specs=[pl.BlockSpec((tile, D_MODEL), lambda i: (i, 0))]
                 + [_full_spec(a.shape) for a in weights],
        out_specs=pl.BlockSpec((tile, D_MODEL), lambda i: (i, 0)),
        out_shape=jax.ShapeDtypeStruct((s, D_MODEL), F32),
        scratch_shapes=[pltpu.VMEM((tile + SUBLANES, D_FF), F32),
                        pltpu.VMEM((SUBLANES, D_FF), F32)],
        compiler_params=pltpu.CompilerParams(dimension_semantics=("arbitrary",),
                                             vmem_limit_bytes=VMEM_LIMIT),
        name="conv_ffn",
    )(x, *weights)


def _pad_cols(a, width):
    return jnp.pad(a, ((0, 0), (0, width - a.shape[1])))


_PACK_ROWS = 128


def _pack_w_in_kernel(w_ref, o_ref):
    w = w_ref[...]
    rows = w.shape[0]

    def padded(a):
        return jnp.concatenate([a, jnp.zeros((rows, LANES - a.shape[1]), a.dtype)], axis=1)

    packed = jnp.concatenate(
        [w[:, 0:1536], w[:, 1552:2064], w[:, 2576:4112], w[:, 4120:7704],
         padded(w[:, 1536:1552]), padded(w[:, 4112:4120])], axis=1)
    o_ref[...] = packed.astype(BF16)


def _pack_w_in(w_in_l):
    d, cols = w_in_l.shape
    return pl.pallas_call(
        _pack_w_in_kernel,
        grid=(d // _PACK_ROWS,),
        in_specs=[pl.BlockSpec((_PACK_ROWS, cols), lambda i: (i, 0))],
        out_specs=pl.BlockSpec((_PACK_ROWS, _W_COLS), lambda i: (i, 0)),
        out_shape=jax.ShapeDtypeStruct((d, _W_COLS), BF16),
        compiler_params=pltpu.CompilerParams(dimension_semantics=("arbitrary",),
                                             vmem_limit_bytes=VMEM_LIMIT),
        name="pack_w_in",
    )(w_in_l)


def _pack_layer(l, p):
    win = _pack_w_in(p["w_in"][l])
    s_u = p["w_in"][l][:, 2064:2576].astype(BF16)
    row = lambda a: a.reshape(1, -1)
    mixer = dict(
        gpre=row(p["norm_mix_pre"][l]), gpost=row(p["norm_mix_post"][l]), win=win,
        wgk=jnp.pad(p["gla_w_gk"][l], ((0, LANES - GLA_RANK), (0, 0))).astype(BF16),
        bgk=row(p["gla_b_gk"][l]), gnorm=row(p["gla_norm"][l]),
        wgp=p["gla_w_proj"][l].astype(BF16),
        wglu=p["s5_w_glu"][l].astype(BF16), bglu=row(p["s5_b_glu"][l]),
        wsp=p["s5_w_proj"][l].astype(BF16),
        cw=p["ml_conv_w"][l], cb=row(p["ml_conv_b"][l]),
        bif=_pad_cols(row(jnp.concatenate([p["ml_b_i"][l], p["ml_b_f"][l]])), LANES),
        wmp=p["ml_w_proj"][l].astype(BF16), wout=p["w_out"][l].astype(BF16))
    s5_w = s_u.T
    s5_prm = _s5_params(p["s5_a_re"][l], p["s5_a_im"][l], p["s5_log_dt"][l], p["s5_b_re"][l],
                        p["s5_b_im"][l], p["s5_c_re"][l], p["s5_c_im"][l], p["s5_d"][l])
    ffn = dict(gpre=row(p["norm_ffn_pre"][l]), gpost=row(p["norm_ffn_post"][l]),
               wup=p["ffn_w_up"][l].astype(BF16), wgate=p["ffn_w_gate"][l].astype(BF16),
               cw=p["ffn_conv_w"][l], cb=row(p["ffn_conv_b"][l]),
               wdown=p["ffn_w_down"][l].astype(BF16))
    return mixer, s5_w, s5_prm, ffn


def _tiles(s):
    return min(1024, s), min(512, s), min(512, s)


def kernel(x, norm_mix_pre, norm_mix_post, norm_ffn_pre, norm_ffn_post, w_in, gla_w_gk, gla_b_gk, gla_norm, gla_w_proj, s5_a_re, s5_a_im, s5_log_dt, s5_b_re, s5_b_im, s5_c_re, s5_c_im, s5_d, s5_w_glu, s5_b_glu, s5_w_proj, ml_conv_w, ml_conv_b, ml_b_i, ml_b_f, ml_w_proj, w_out, ffn_w_up, ffn_w_gate, ffn_conv_w, ffn_conv_b, ffn_w_down):
    p = dict(norm_mix_pre=norm_mix_pre, norm_mix_post=norm_mix_post, norm_ffn_pre=norm_ffn_pre,
             norm_ffn_post=norm_ffn_post, w_in=w_in, gla_w_gk=gla_w_gk, gla_b_gk=gla_b_gk,
             gla_norm=gla_norm, gla_w_proj=gla_w_proj, s5_a_re=s5_a_re, s5_a_im=s5_a_im,
             s5_log_dt=s5_log_dt, s5_b_re=s5_b_re, s5_b_im=s5_b_im, s5_c_re=s5_c_re,
             s5_c_im=s5_c_im, s5_d=s5_d, s5_w_glu=s5_w_glu, s5_b_glu=s5_b_glu, s5_w_proj=s5_w_proj,
             ml_conv_w=ml_conv_w, ml_conv_b=ml_conv_b, ml_b_i=ml_b_i, ml_b_f=ml_b_f,
             ml_w_proj=ml_w_proj, w_out=w_out, ffn_w_up=ffn_w_up, ffn_w_gate=ffn_w_gate,
             ffn_conv_w=ffn_conv_w, ffn_conv_b=ffn_conv_b, ffn_w_down=ffn_w_down)
    bsz, s, _ = x.shape
    assert bsz == 1 and s % (S5_BLOCK * SUBLANES) == 0
    t_s5, t_mix, t_ffn = _tiles(s)
    h = x[0]
    for l in range(w_in.shape[0]):
        mixer_w, s5_w, s5_prm, ffn_w = _pack_layer(l, p)
        u_t = _s5_in_proj(h, mixer_w["gpre"], s5_w, t_s5)
        y_t = _s5_core(u_t, s5_prm)
        h = _mixer(h, y_t, mixer_w, t_mix)
        h = _ffn(h, ffn_w, t_ffn)
    return h[None]
```

```python
import functools
import math

import jax
import jax.numpy as jnp
from jax import lax
from jax.experimental import pallas as pl
from jax.experimental.pallas import tpu as pltpu

F32 = jnp.float32
BF16 = jnp.bfloat16

D_MODEL = 1024
CHUNK = 64
NORM_EPS = 1e-6
BRANCH = 512
HEADS = 4
HEAD_DIM = BRANCH // HEADS
GLA_RANK = 16
GLA_GATE_NORM = 16.0
S5_GROUP_CH = 16
S5_GROUPS = BRANCH // S5_GROUP_CH
S5_STATE = 64
S5_BLOCK = 128
ML_CONV = 4
D_FF = 2816
FFN_CONV = 3
LANES = 128
SUBLANES = 8

VMEM_LIMIT = 56 * 1024 * 1024


def _rms_norm(x, gain):
    return x * lax.rsqrt(jnp.mean(x * x, axis=-1, keepdims=True) + NORM_EPS) * gain


def _dot(a, b):
    return jnp.dot(a, b, preferred_element_type=F32)


def _dot_nt(a, b):
    return lax.dot_general(a, b, (((1,), (1,)), ((), ())), preferred_element_type=F32)


def _log_sigmoid(x):
    return jnp.minimum(x, 0.0) - jnp.log(1.0 + jnp.exp(-jnp.abs(x)))


def _sigmoid(x):
    return 1.0 / (1.0 + jnp.exp(-x))


def _silu(x):
    return x * _sigmoid(x)


def _gelu_tanh(x):
    c = math.sqrt(2.0 / math.pi)
    return x * (0.5 * (1.0 + jnp.tanh(c * (x + 0.044715 * (x * x * x)))))


def _causal_conv_rows(x, prev, w, bias):
    k = w.shape[0]
    row = lax.broadcasted_iota(jnp.int32, (SUBLANES, x.shape[1]), 0)
    acc = w[0:1, :] * x
    for step in range(1, k):
        first = w[0:1, :] * prev[SUBLANES - step:SUBLANES - step + 1, :]
        for j in range(1, step):
            first = first + w[j:j + 1, :] * prev[SUBLANES - step + j:SUBLANES - step + j + 1, :]
        rolled = pltpu.roll(acc, 1, 0)
        head = jnp.where(row == 0, first, rolled[0:SUBLANES, :])
        acc = jnp.concatenate([head, rolled[SUBLANES:, :]], axis=0) + w[step:step + 1, :] * x
    return acc + bias


def _full_spec(shape):
    zeros = (0,) * len(shape)
    return pl.BlockSpec(shape, lambda *_: zeros, pipeline_mode=pl.Buffered(1))


def _s5_in_kernel(x_ref, gain_ref, w_ref, o_ref):
    u = _rms_norm(x_ref[...], gain_ref[...]).astype(BF16)
    o_ref[...] = _dot_nt(w_ref[...], u).astype(BF16)


def _s5_in_proj(x, gain, w_t, tile):
    s = x.shape[0]
    return pl.pallas_call(
        _s5_in_kernel,
        grid=(s // tile,),
        in_specs=[pl.BlockSpec((tile, D_MODEL), lambda i: (i, 0)),
                  _full_spec((1, D_MODEL)),
                  _full_spec((BRANCH, D_MODEL))],
        out_specs=pl.BlockSpec((BRANCH, tile), lambda i: (0, i)),
        out_shape=jax.ShapeDtypeStruct((BRANCH, s), BF16),
        compiler_params=pltpu.CompilerParams(dimension_semantics=("arbitrary",),
                                             vmem_limit_bytes=VMEM_LIMIT),
        name="s5_in_proj",
    )(x, gain, w_t)


def _cmul(xr, xi, yr, yi):
    return xr * yr - xi * yi, xr * yi + xi * yr


def _s5_kernel(u_ref, are_c_ref, aim_c_ref, are_r_ref, aim_r_ref, ldt_ref,
               b_re_ref, b_im_ref, bt_re_ref, bt_im_ref, c_re_ref, c_im_ref, ct_re_ref, ct_im_ref, d_ref,
               o_ref, k_sc, ta_sc, tb_sc, acc_sc, win_sc, m_sc):
    nch, nblk, blk = u_ref.shape
    half = S5_STATE
    blk_f = float(blk)
    dt = jnp.exp(ldt_ref[...])

    def zoh_factor(are, aim, abr, abi):
        den = are * are + aim * aim
        zr = abr - 1.0
        return (zr * are + abi * aim) / den, (abi * are - zr * aim) / den

    are_r, aim_r = are_r_ref[...], aim_r_ref[...]
    rho_r, th_r = dt * are_r, dt * aim_r
    mag_r = jnp.exp(rho_r)
    fr_r, fi_r = zoh_factor(are_r, aim_r, mag_r * jnp.cos(th_r), mag_r * jnp.sin(th_r))
    bbr, bbi = _cmul(fr_r, fi_r, bt_re_ref[...], bt_im_ref[...])

    are_c, aim_c = are_c_ref[...], aim_c_ref[...]
    rho_c, th_c = dt * are_c, dt * aim_c
    lag_l = lax.broadcasted_iota(jnp.int32, (half, blk), 1).astype(F32)
    pmag = jnp.exp(rho_c * lag_l)
    p_re, p_im = pmag * jnp.cos(th_c * lag_l), pmag * jnp.sin(th_c * lag_l)
    abr_c, abi_c = p_re[:, 1:2], p_im[:, 1:2]
    fr_c, fi_c = zoh_factor(are_c, aim_c, abr_c, abi_c)
    bcr, bci = _cmul(fr_c, fi_c, b_re_ref[...], b_im_ref[...])

    c_re, c_im = c_re_ref[...], c_im_ref[...]
    cb_rows = []
    for hi in range(nch):
        cbr, cbi = _cmul(c_re, c_im, bbr[hi:hi + 1, :], bbi[hi:hi + 1, :])
        cb_rows.append(jnp.concatenate([cbr, cbi], axis=1))
    cb = jnp.concatenate(cb_rows, axis=0)
    pstack = jnp.concatenate([p_re, -p_im], axis=0)
    kern = jnp.dot(cb, pstack, preferred_element_type=F32, precision=lax.Precision.HIGHEST)
    rr = lax.broadcasted_iota(jnp.int32, (nch * nch, blk), 0)
    ll = lax.broadcasted_iota(jnp.int32, (nch * nch, blk), 1)
    d_col = jnp.concatenate([d_ref[...]] * nch, axis=0)
    diag = ((rr >> 4) == (rr & (nch - 1))) & (ll == 0)
    k_sc[...] = kern + jnp.where(diag, d_col, 0.0)

    lag_s = blk_f - 1.0 - lag_l
    qmag = jnp.exp(rho_c * lag_s)
    q_re, q_im = qmag * jnp.cos(th_c * lag_s), qmag * jnp.sin(th_c * lag_s)
    for hi in range(nch):
        wr, wi = _cmul(q_re, q_im, bcr[:, hi:hi + 1], bci[:, hi:hi + 1])
        win_sc[:, hi * blk:(hi + 1) * blk] = jnp.concatenate([wr, wi], axis=0).astype(BF16)

    p1_re, p1_im = _cmul(p_re, p_im, abr_c, abi_c)
    ct_re, ct_im = ct_re_ref[...], ct_im_ref[...]
    for ho in range(nch):
        mr, mi = _cmul(p1_re, p1_im, ct_re[:, ho:ho + 1], ct_im[:, ho:ho + 1])
        m_sc[:, ho * blk:(ho + 1) * blk] = jnp.concatenate([mr, -mi], axis=0).astype(BF16)

    ucat = jnp.concatenate([u_ref[h] for h in range(nch)], axis=1)

    x = _dot_nt(ucat, win_sc[...])
    wmag = jnp.exp(rho_r * blk_f)
    w_re, w_im = wmag * jnp.cos(th_r * blk_f), wmag * jnp.sin(th_r * blk_f)
    row = lax.broadcasted_iota(jnp.int32, (nblk, 2 * half), 0)

    def shift_rows(a, k):
        return jnp.where(row >= k, pltpu.roll(a, k, 0), 0.0)

    def state_mul(a, wr, wi):
        return (a * jnp.concatenate([wr, wr], axis=1)
                + pltpu.roll(a, half, 1) * jnp.concatenate([-wi, wi], axis=1))

    step = 1
    while step < nblk:
        x = x + state_mul(shift_rows(x, step), w_re, w_im)
        w_re, w_im = _cmul(w_re, w_im, w_re, w_im)
        step *= 2
    acc_sc[...] = _dot(shift_rows(x, 1).astype(BF16), m_sc[...])

    srow = lax.broadcasted_iota(jnp.int32, (blk, blk), 0)
    lcol = lax.broadcasted_iota(jnp.int32, (blk, blk), 1)
    lower = lcol >= srow

    def build(pair, slab):
        for sub in range(2):
            for ho in range(nch):
                krow = k_sc[pl.ds((2 * pair + sub) * nch + ho, 1), :]
                tile = pltpu.roll(jnp.broadcast_to(krow, (blk, blk)), 0, 1, stride=1, stride_axis=0)
                slab[sub * blk:(sub + 1) * blk, ho * blk:(ho + 1) * blk] = (
                    jnp.where(lower, tile, 0.0).astype(BF16))

    def apply(pair, slab):
        u_pair = jnp.concatenate([u_ref[2 * pair], u_ref[2 * pair + 1]], axis=1)
        acc_sc[...] += _dot(u_pair, slab[...])

    npair = nch // 2
    build(0, ta_sc)

    def two_pairs(q, carry):
        build(2 * q + 1, tb_sc)
        apply(2 * q, ta_sc)
        build(jnp.minimum(2 * q + 2, npair - 1), ta_sc)
        apply(2 * q + 1, tb_sc)
        return carry

    lax.fori_loop(0, npair // 2, two_pairs, 0)
    for ho in range(nch):
        o_ref[ho] = acc_sc[:, ho * blk:(ho + 1) * blk]


def _s5_core(u_t, prm):
    s = u_t.shape[1]
    nblk = s // S5_BLOCK
    u4 = u_t.reshape(S5_GROUPS, S5_GROUP_CH, nblk, S5_BLOCK)

    def gspec(*dims):
        zeros = (0,) * len(dims)
        return pl.BlockSpec((None,) + dims, lambda g: (g,) + zeros)

    width = S5_GROUP_CH * S5_BLOCK
    out = pl.pallas_call(
        _s5_kernel,
        grid=(S5_GROUPS,),
        in_specs=[gspec(S5_GROUP_CH, nblk, S5_BLOCK),
                  gspec(S5_STATE, 1), gspec(S5_STATE, 1), gspec(1, S5_STATE), gspec(1, S5_STATE),
                  gspec(1, 1),
                  gspec(S5_STATE, S5_GROUP_CH), gspec(S5_STATE, S5_GROUP_CH),
                  gspec(S5_GROUP_CH, S5_STATE), gspec(S5_GROUP_CH, S5_STATE),
                  gspec(S5_GROUP_CH, S5_STATE), gspec(S5_GROUP_CH, S5_STATE),
                  gspec(S5_STATE, S5_GROUP_CH), gspec(S5_STATE, S5_GROUP_CH),
                  gspec(S5_GROUP_CH, 1)],
        out_specs=gspec(S5_GROUP_CH, nblk, S5_BLOCK),
        out_shape=jax.ShapeDtypeStruct(u4.shape, F32),
        scratch_shapes=[pltpu.VMEM((S5_GROUP_CH * S5_GROUP_CH, S5_BLOCK), F32),
                        pltpu.VMEM((2 * S5_BLOCK, width), BF16),
                        pltpu.VMEM((2 * S5_BLOCK, width), BF16),
                        pltpu.VMEM((nblk, width), F32),
                        pltpu.VMEM((2 * S5_STATE, width), BF16),
                        pltpu.VMEM((2 * S5_STATE, width), BF16)],
        compiler_params=pltpu.CompilerParams(dimension_semantics=("arbitrary",),
                                             vmem_limit_bytes=VMEM_LIMIT),
        name="s5_core",
    )(u4, *prm)
    return out.reshape(BRANCH, s)


def _s5_params(a_re, a_im, log_dt, b_re, b_im, c_re, c_im, d):
    g = S5_GROUPS
    return (a_re.reshape(g, S5_STATE, 1), a_im.reshape(g, S5_STATE, 1),
            a_re.reshape(g, 1, S5_STATE), a_im.reshape(g, 1, S5_STATE),
            log_dt.reshape(g, 1, 1),
            b_re, b_im,
            jnp.swapaxes(b_re, 1, 2), jnp.swapaxes(b_im, 1, 2),
            c_re, c_im,
            jnp.swapaxes(c_re, 1, 2), jnp.swapaxes(c_im, 1, 2),
            d.reshape(g, S5_GROUP_CH, 1))


_C_GQ, _C_GK, _C_GV, _C_GR = 0, 512, 1024, 1536
_C_MQ, _C_MK, _C_MV, _C_MO = 2048, 2560, 3072, 3584
_C_G0 = 4096
_C_LR = _C_G0 + 3 * D_MODEL
_C_IF = _C_LR + LANES
_W_COLS = _C_IF + LANES

GROUP = 4 * CHUNK
_LOG2_CHUNK = CHUNK.bit_length() - 1
_LOG2_HEAD_DIM = HEAD_DIM.bit_length() - 1
_GATE_SLICE = 256
_N_GATE_SLICES = 3 * D_MODEL // _GATE_SLICE


def _group_masks():
    r = lax.broadcasted_iota(jnp.int32, (GROUP, GROUP), 0)
    c = lax.broadcasted_iota(jnp.int32, (GROUP, GROUP), 1)
    causal = ((r >> _LOG2_CHUNK) == (c >> _LOG2_CHUNK)) & (r >= c)
    rr = lax.broadcasted_iota(jnp.int32, (4 * HEAD_DIM, GROUP), 0)
    cc = lax.broadcasted_iota(jnp.int32, (4 * HEAD_DIM, GROUP), 1)
    stacked = (rr >> _LOG2_HEAD_DIM) == (cc >> _LOG2_CHUNK)
    return causal, stacked


def _cumsum_chunks(tri, a):
    hi = a.astype(BF16)
    mid = (a - hi.astype(F32)).astype(BF16)
    return _dot(tri, hi) + _dot(tri, mid)


def _chunk_rows(c):
    return slice(c * CHUNK, (c + 1) * CHUNK)


def _per_chunk_sums(x_t, stacked, rhs):
    lhs = jnp.where(stacked, jnp.concatenate([x_t] * 4, axis=0), 0.0).astype(BF16)
    return _dot(lhs, rhs)


def _gla_group(q, k, v, la, states, causal, tri, stacked, filler):
    hd = HEAD_DIM
    b = _cumsum_chunks(tri, la)
    last = [b[c * CHUNK + CHUNK - 1:(c + 1) * CHUNK, :] for c in range(4)]
    b_last = jnp.concatenate([jnp.broadcast_to(r, (CHUNK, BRANCH)) for r in last], axis=0)
    q_dec = (q * (hd ** -0.5) * jnp.exp(b)).astype(BF16)
    k_inv = (k * jnp.exp(-b)).astype(BF16)
    k_end = (k * jnp.exp(b_last - b)).astype(BF16)
    vb = v.astype(BF16)
    decay = [jnp.exp(r) for r in last]
    outs, new_states = [], []
    for h in range(HEADS):
        sl = slice(h * hd, (h + 1) * hd)
        filler()
        scores = jnp.where(causal, _dot_nt(q_dec[:, sl], k_inv[:, sl]), 0.0).astype(BF16)
        intra = _dot(scores, vb[:, sl])
        kv = _per_chunk_sums(v[:, sl].T, stacked, k_end[:, sl])
        st = states[h]
        inter = []
        for c in range(4):
            inter.append(_dot_nt(q_dec[_chunk_rows(c), sl], st.astype(BF16)))
            st = st * decay[c][:, sl] + kv[c * hd:(c + 1) * hd, :]
        new_states.append(st)
        outs.append(intra + jnp.concatenate(inter, axis=0))
    return jnp.concatenate(outs, axis=1), new_states


def _mlstm_group(q, k, v, gates, states, m_run, causal, tri, stacked, filler):
    hd = HEAD_DIM
    cs = _cumsum_chunks(tri, gates)
    log_i = pltpu.roll(gates, HEADS, 1)
    w_rows, m_prev_rows, scale = [], [], []
    for c in range(4):
        rows = _chunk_rows(c)
        b_last = cs[c * CHUNK + CHUNK - 1:(c + 1) * CHUNK, :]
        g_end = b_last - cs[rows, :] + log_i[rows, :]
        m_new = jnp.maximum(b_last + m_run, jnp.max(g_end, axis=0, keepdims=True))
        w_rows.append(jnp.exp(g_end - m_new))
        scale.append(jnp.exp(b_last + m_run - m_new))
        m_prev_rows.append(jnp.broadcast_to(m_run, (CHUNK, LANES)))
        m_run = m_new
    w_all = jnp.concatenate(w_rows, axis=0)
    m_inter = cs + jnp.concatenate(m_prev_rows, axis=0)
    lane = lax.broadcasted_iota(jnp.int32, (GROUP, LANES), 1)
    z_t = jnp.where(lane < HEADS, gates, cs).T
    ones = jnp.ones((GROUP, hd), BF16)
    outs, new_states = [], []
    for h in range(HEADS):
        sl = slice(h * hd, (h + 1) * hd)
        col = HEADS + h
        filler()
        log_w = jnp.where(causal, cs[:, col:col + 1] - z_t[col:col + 1, :] + z_t[h:h + 1, :], -jnp.inf)
        m_int = m_inter[:, col:col + 1]
        m = jnp.maximum(m_int, jnp.max(log_w, axis=-1, keepdims=True))
        s_inter = jnp.exp(m_int - m)
        qb = q[:, sl].astype(BF16)
        kh = k[:, sl]
        v_aug = jnp.concatenate([v[:, sl].astype(BF16), ones], axis=1)
        qk = (_dot_nt(qb, kh.astype(BF16)) * jnp.exp(log_w - m)).astype(BF16)
        intra = _dot(qk, v_aug)
        kw_t = (kh * w_all[:, col:col + 1]).T
        kv = _per_chunk_sums(kw_t, stacked, v_aug)
        st = states[h]
        inter = []
        for c in range(4):
            inter.append(_dot(qb[_chunk_rows(c), :], st.astype(BF16)))
            st = scale[c][:, col:col + 1] * st + kv[c * hd:(c + 1) * hd, :]
        new_states.append(st)
        tot = s_inter * jnp.concatenate(inter, axis=0) + intra
        outs.append(tot[:, :hd] / jnp.maximum(jnp.abs(tot[:, hd:]), jnp.exp(-m)))
    return jnp.concatenate(outs, axis=1), new_states, m_run


def _mixer_kernel(x_ref, yst_ref, gpre_ref, gpost_ref, win_ref,
                  wgk_ref, bgk_ref, gnorm_ref, wgp_ref,
                  wglu_ref, bglu_ref, wsp_ref,
                  cw_ref, cb_ref, bif_ref, wmp_ref, wout_ref,
                  o_ref,
                  gst_sc, mc_sc, mm_sc, carry_sc):
    tm = x_ref.shape[0]
    ngroup = tm // GROUP
    hd = HEAD_DIM

    @pl.when(pl.program_id(0) == 0)
    def _init():
        gst_sc[...] = jnp.zeros_like(gst_sc)
        mc_sc[...] = jnp.zeros_like(mc_sc)
        mm_sc[...] = jnp.zeros_like(mm_sc)
        carry_sc[...] = jnp.zeros_like(carry_sc)

    x = x_ref[...]
    u = _rms_norm(x, gpre_ref[...]).astype(BF16)

    def proj(c0, width):
        return _dot(u, win_ref[:, c0:c0 + width])

    causal, stacked = _group_masks()
    tri = causal.astype(BF16)

    done, queue = {}, []

    def filler(n=1):
        for _ in range(n):
            if queue:
                name, fn = queue.pop(0)
                done[name] = fn()

    def need(name):
        while name not in done:
            filler()
        return done[name]

    def s5_tail():
        ys = _gelu_tanh(yst_ref[...].T)
        glu = _sigmoid(_dot(ys.astype(BF16), wglu_ref[...]) + bglu_ref[...])
        return _dot((ys * glu).astype(BF16), wsp_ref[...])

    queue += [("mq", lambda: proj(_C_MQ, BRANCH)), ("mk", lambda: proj(_C_MK, BRANCH)),
              ("mv", lambda: proj(_C_MV, BRANCH)),
              ("pre_if", lambda: proj(_C_IF, LANES) + bif_ref[...]),
              ("mo", lambda: _sigmoid(proj(_C_MO, BRANCH))),
              ("g_r", lambda: _silu(proj(_C_GR, BRANCH)))]
    queue += [("gate%d" % i, functools.partial(
        lambda c0: _sigmoid(proj(c0, _GATE_SLICE)), _C_G0 + i * _GATE_SLICE))
        for i in range(_N_GATE_SLICES)]
    queue.insert(8, ("y_s5", s5_tail))
    fill_gla = -(-8 // (HEADS * ngroup))
    fill_ml = -(-(len(queue) + 1 - 8) // (HEADS * ngroup))

    gq, gk, gv = proj(_C_GQ, BRANCH), proj(_C_GK, BRANCH), proj(_C_GV, BRANCH)
    g_lr = proj(_C_LR, LANES).astype(BF16)
    log_a = _log_sigmoid(_dot(g_lr, wgk_ref[...]) + bgk_ref[...]) / GLA_GATE_NORM
    states = [gst_sc[h] for h in range(HEADS)]
    o_groups = []
    for g in range(ngroup):
        r = slice(g * GROUP, (g + 1) * GROUP)
        o_g, states = _gla_group(gq[r], gk[r], gv[r], log_a[r], states, causal, tri, stacked,
                                 functools.partial(filler, fill_gla))
        o_groups.append(o_g)
    for h in range(HEADS):
        gst_sc[h] = states[h]
    o = jnp.concatenate(o_groups, axis=0)
    gnorm = gnorm_ref[...]
    o_heads = [_rms_norm(o[:, h * hd:(h + 1) * hd], gnorm) for h in range(HEADS)]
    o_gated = (jnp.concatenate(o_heads, axis=1) * need("g_r")).astype(BF16)
    queue.insert(0, ("y_gla", lambda: _dot(o_gated, wgp_ref[...])))

    qk_pre = jnp.concatenate([need("mq"), need("mk")], axis=1)
    prev = carry_sc[...]
    carry_sc[...] = qk_pre[tm - SUBLANES:, :]
    conv = _causal_conv_rows(qk_pre, prev, cw_ref[...], cb_ref[...])
    qk_act = _silu(conv)
    mq = qk_act[:, 0:BRANCH]
    mk = qk_act[:, BRANCH:] * (hd ** -0.5)
    mv = need("mv")
    pre_if = need("pre_if")
    lane = lax.broadcasted_iota(jnp.int32, (tm, LANES), 1)
    gates = jnp.where(lane < HEADS, pre_if, _log_sigmoid(pre_if))
    states = [mc_sc[h] for h in range(HEADS)]
    m_run = mm_sc[0:1, :]
    h_groups = []
    for g in range(ngroup):
        r = slice(g * GROUP, (g + 1) * GROUP)
        h_g, states, m_run = _mlstm_group(mq[r], mk[r], mv[r], gates[r], states, m_run,
                                          causal, tri, stacked, functools.partial(filler, fill_ml))
        h_groups.append(h_g)
    for h in range(HEADS):
        mc_sc[h] = states[h]
    mm_sc[...] = jnp.broadcast_to(m_run, mm_sc.shape)
    hid = (need("mo") * jnp.concatenate(h_groups, axis=0)).astype(BF16)
    y_ml = _dot(hid, wmp_ref[...])

    per_branch = D_MODEL // _GATE_SLICE
    branch_gate = [jnp.concatenate([need("gate%d" % (b * per_branch + i)) for i in range(per_branch)],
                                   axis=1) for b in range(3)]
    merged = branch_gate[0] * need("y_gla") + branch_gate[1] * need("y_s5") + branch_gate[2] * y_ml
    mix = _dot(merged.astype(BF16), wout_ref[...])
    o_ref[...] = x + _rms_norm(mix, gpost_ref[...])


def _mixer(x, ys_t, w, tile):
    s = x.shape[0]
    weights = (w["gpre"], w["gpost"], w["win"], w["wgk"], w["bgk"], w["gnorm"], w["wgp"],
               w["wglu"], w["bglu"], w["wsp"], w["cw"], w["cb"], w["bif"], w["wmp"], w["wout"])
    return pl.pallas_call(
        _mixer_kernel,
        grid=(s // tile,),
        in_specs=[pl.BlockSpec((tile, D_MODEL), lambda i: (i, 0)),
                  pl.BlockSpec((BRANCH, tile), lambda i: (0, i))]
                 + [_full_spec(a.shape) for a in weights],
        out_specs=pl.BlockSpec((tile, D_MODEL), lambda i: (i, 0)),
        out_shape=jax.ShapeDtypeStruct((s, D_MODEL), F32),
        scratch_shapes=[pltpu.VMEM((HEADS, HEAD_DIM, HEAD_DIM), F32),
                        pltpu.VMEM((HEADS, HEAD_DIM, 2 * HEAD_DIM), F32),
                        pltpu.VMEM((SUBLANES, LANES), F32),
                        pltpu.VMEM((SUBLANES, 2 * BRANCH), F32)],
        compiler_params=pltpu.CompilerParams(dimension_semantics=("arbitrary",),
                                             vmem_limit_bytes=VMEM_LIMIT),
        name="mixer",
    )(x, ys_t, *weights)


_FFN_ROW_BLOCK = 256


def _ffn_kernel(x_ref, gpre_ref, gpost_ref, wup_ref, wgate_ref, cw_ref, cb_ref, wdown_ref,
                o_ref, a_sc, carry_sc):
    tm = x_ref.shape[0]
    rb = min(_FFN_ROW_BLOCK, tm)

    @pl.when(pl.program_id(0) == 0)
    def _zero_carry():
        carry_sc[...] = jnp.zeros_like(carry_sc)

    a_sc[0:SUBLANES, :] = carry_sc[...]
    cw = cw_ref[...]
    for r in range(tm // rb):
        rows = slice(r * rb, (r + 1) * rb)
        x = x_ref[rows, :]
        u = _rms_norm(x, gpre_ref[...]).astype(BF16)
        a = _dot(u, wup_ref[...])
        gate = _dot(u, wgate_ref[...])
        a_sc[SUBLANES + r * rb:SUBLANES + (r + 1) * rb, :] = a
        conv = cb_ref[...] + cw[FFN_CONV - 1:FFN_CONV, :] * a
        for j in range(1, FFN_CONV):
            lo = SUBLANES + r * rb - j
            conv = conv + cw[FFN_CONV - 1 - j:FFN_CONV - j, :] * a_sc[lo:lo + rb, :]
        h = (_gelu_tanh(conv) * gate).astype(BF16)
        o_ref[rows, :] = x + _rms_norm(_dot(h, wdown_ref[...]), gpost_ref[...])

    carry_sc[...] = a_sc[tm:tm + SUBLANES, :]


def _ffn(x, w, tile):
    s = x.shape[0]
    weights = (w["gpre"], w["gpost"], w["wup"], w["wgate"], w["cw"], w["cb"], w["wdown"])
    return pl.pallas_call(
        _ffn_kernel,
        grid=(s // tile,),
        in_specs=[pl.BlockSpec((tile, D_MODEL), lambda i: (i, 0))]
                 + [_full_spec(a.shape) for a in weights],
        out_specs=pl.BlockSpec((tile, D_MODEL), lambda i: (i, 0)),
        out_shape=jax.ShapeDtypeStruct((s, D_MODEL), F32),
        scratch_shapes=[pltpu.VMEM((tile + SUBLANES, D_FF), F32),
                        pltpu.VMEM((SUBLANES, D_FF), F32)],
        compiler_params=pltpu.CompilerParams(dimension_semantics=("arbitrary",),
                                             vmem_limit_bytes=VMEM_LIMIT),
        name="conv_ffn",
    )(x, *weights)


def _pad_cols(a, width):
    return jnp.pad(a, ((0, 0), (0, width - a.shape[1])))


_PACK_ROWS = 128


def _pack_w_in_kernel(w_ref, o_ref, s5_ref):
    w = w_ref[...]
    rows = w.shape[0]

    def padded(a):
        return jnp.concatenate([a, jnp.zeros((rows, LANES - a.shape[1]), a.dtype)], axis=1)

    packed = jnp.concatenate(
        [w[:, 0:1536], w[:, 1552:2064], w[:, 2576:4112], w[:, 4120:7704],
         padded(w[:, 1536:1552]), padded(w[:, 4112:4120])], axis=1)
    o_ref[...] = packed.astype(BF16)
    s5_ref[...] = w[:, 2064:2576].T.astype(BF16)


def _pack_w_in(w_in_l):
    d, cols = w_in_l.shape
    return pl.pallas_call(
        _pack_w_in_kernel,
        grid=(d // _PACK_ROWS,),
        in_specs=[pl.BlockSpec((_PACK_ROWS, cols), lambda i: (i, 0))],
        out_specs=[pl.BlockSpec((_PACK_ROWS, _W_COLS), lambda i: (i, 0)),
                   pl.BlockSpec((BRANCH, _PACK_ROWS), lambda i: (0, i))],
        out_shape=[jax.ShapeDtypeStruct((d, _W_COLS), BF16),
                   jax.ShapeDtypeStruct((BRANCH, d), BF16)],
        compiler_params=pltpu.CompilerParams(dimension_semantics=("arbitrary",),
                                             vmem_limit_bytes=VMEM_LIMIT),
        name="pack_w_in",
    )(w_in_l)


def _pack_layer(l, p):
    win, s5_w = _pack_w_in(p["w_in"][l])
    row = lambda a: a.reshape(1, -1)
    mixer = dict(
        gpre=row(p["norm_mix_pre"][l]), gpost=row(p["norm_mix_post"][l]), win=win,
        wgk=jnp.pad(p["gla_w_gk"][l], ((0, LANES - GLA_RANK), (0, 0))).astype(BF16),
        bgk=row(p["gla_b_gk"][l]), gnorm=row(p["gla_norm"][l]),
        wgp=p["gla_w_proj"][l].astype(BF16),
        wglu=p["s5_w_glu"][l].astype(BF16), bglu=row(p["s5_b_glu"][l]),
        wsp=p["s5_w_proj"][l].astype(BF16),
        cw=p["ml_conv_w"][l], cb=row(p["ml_conv_b"][l]),
        bif=_pad_cols(row(jnp.concatenate([p["ml_b_i"][l], p["ml_b_f"][l]])), LANES),
        wmp=p["ml_w_proj"][l].astype(BF16), wout=p["w_out"][l].astype(BF16))
    s5_prm = _s5_params(p["s5_a_re"][l], p["s5_a_im"][l], p["s5_log_dt"][l], p["s5_b_re"][l],
                        p["s5_b_im"][l], p["s5_c_re"][l], p["s5_c_im"][l], p["s5_d"][l])
    ffn = dict(gpre=row(p["norm_ffn_pre"][l]), gpost=row(p["norm_ffn_post"][l]),
               wup=p["ffn_w_up"][l].astype(BF16), wgate=p["ffn_w_gate"][l].astype(BF16),
               cw=p["ffn_conv_w"][l], cb=row(p["ffn_conv_b"][l]),
               wdown=p["ffn_w_down"][l].astype(BF16))
    return mixer, s5_w, s5_prm, ffn


def _tiles(s):
    return min(1024, s), min(512, s), min(512, s)


def kernel(x, norm_mix_pre, norm_mix_post, norm_ffn_pre, norm_ffn_post, w_in, gla_w_gk, gla_b_gk, gla_norm, gla_w_proj, s5_a_re, s5_a_im, s5_log_dt, s5_b_re, s5_b_im, s5_c_re, s5_c_im, s5_d, s5_w_glu, s5_b_glu, s5_w_proj, ml_conv_w, ml_conv_b, ml_b_i, ml_b_f, ml_w_proj, w_out, ffn_w_up, ffn_w_gate, ffn_conv_w, ffn_conv_b, ffn_w_down):
    p = dict(norm_mix_pre=norm_mix_pre, norm_mix_post=norm_mix_post, norm_ffn_pre=norm_ffn_pre,
             norm_ffn_post=norm_ffn_post, w_in=w_in, gla_w_gk=gla_w_gk, gla_b_gk=gla_b_gk,
             gla_norm=gla_norm, gla_w_proj=gla_w_proj, s5_a_re=s5_a_re, s5_a_im=s5_a_im,
             s5_log_dt=s5_log_dt, s5_b_re=s5_b_re, s5_b_im=s5_b_im, s5_c_re=s5_c_re,
             s5_c_im=s5_c_im, s5_d=s5_d, s5_w_glu=s5_w_glu, s5_b_glu=s5_b_glu, s5_w_proj=s5_w_proj,
             ml_conv_w=ml_conv_w, ml_conv_b=ml_conv_b, ml_b_i=ml_b_i, ml_b_f=ml_b_f,
             ml_w_proj=ml_w_proj, w_out=w_out, ffn_w_up=ffn_w_up, ffn_w_gate=ffn_w_gate,
             ffn_conv_w=ffn_conv_w, ffn_conv_b=ffn_conv_b, ffn_w_down=ffn_w_down)
    bsz, s, _ = x.shape
    assert bsz == 1 and s % (S5_BLOCK * SUBLANES) == 0
    t_s5, t_mix, t_ffn = _tiles(s)
    h = x[0]
    for l in range(w_in.shape[0]):
        mixer_w, s5_w, s5_prm, ffn_w = _pack_layer(l, p)
        u_t = _s5_in_proj(h, mixer_w["gpre"], s5_w, t_s5)
        y_t = _s5_core(u_t, s5_prm)
        h = _mixer(h, y_t, mixer_w, t_mix)
        h = _ffn(h, ffn_w, t_ffn)
    return h[None]
```

```python
import functools
import math

import jax
import jax.numpy as jnp
from jax import lax
from jax.experimental import pallas as pl
from jax.experimental.pallas import tpu as pltpu

F32 = jnp.float32
BF16 = jnp.bfloat16

D_MODEL = 1024
CHUNK = 64
NORM_EPS = 1e-6
BRANCH = 512
HEADS = 4
HEAD_DIM = BRANCH // HEADS
GLA_RANK = 16
GLA_GATE_NORM = 16.0
S5_GROUP_CH = 16
S5_GROUPS = BRANCH // S5_GROUP_CH
S5_STATE = 64
S5_BLOCK = 128
ML_CONV = 4
D_FF = 2816
FFN_CONV = 3
LANES = 128
SUBLANES = 8

VMEM_LIMIT = 56 * 1024 * 1024


def _rms_norm(x, gain):
    return x * lax.rsqrt(jnp.mean(x * x, axis=-1, keepdims=True) + NORM_EPS) * gain


def _dot(a, b):
    return jnp.dot(a, b, preferred_element_type=F32)


def _dot_nt(a, b):
    return lax.dot_general(a, b, (((1,), (1,)), ((), ())), preferred_element_type=F32)


def _log_sigmoid(x):
    return jnp.minimum(x, 0.0) - jnp.log(1.0 + jnp.exp(-jnp.abs(x)))


def _sigmoid(x):
    return 1.0 / (1.0 + jnp.exp(-x))


def _silu(x):
    return x * _sigmoid(x)


def _gelu_tanh(x):
    c = math.sqrt(2.0 / math.pi)
    return x * (0.5 * (1.0 + jnp.tanh(c * (x + 0.044715 * (x * x * x)))))


def _causal_conv_rows(x, prev, w, bias):
    k = w.shape[0]
    row = lax.broadcasted_iota(jnp.int32, (SUBLANES, x.shape[1]), 0)
    acc = w[0:1, :] * x
    for step in range(1, k):
        first = w[0:1, :] * prev[SUBLANES - step:SUBLANES - step + 1, :]
        for j in range(1, step):
            first = first + w[j:j + 1, :] * prev[SUBLANES - step + j:SUBLANES - step + j + 1, :]
        rolled = pltpu.roll(acc, 1, 0)
        head = jnp.where(row == 0, first, rolled[0:SUBLANES, :])
        acc = jnp.concatenate([head, rolled[SUBLANES:, :]], axis=0) + w[step:step + 1, :] * x
    return acc + bias


def _full_spec(shape):
    zeros = (0,) * len(shape)
    return pl.BlockSpec(shape, lambda *_: zeros, pipeline_mode=pl.Buffered(1))


def _weight_spec(a, layer):
    if a.ndim != 3:
        return _full_spec(a.shape)
    return pl.BlockSpec((None,) + a.shape[1:], lambda *_: (layer, 0, 0), pipeline_mode=pl.Buffered(1))


def _s5_in_kernel(x_ref, gain_ref, w_ref, o_ref):
    u = _rms_norm(x_ref[...], gain_ref[...]).astype(BF16)
    o_ref[...] = _dot_nt(w_ref[...], u).astype(BF16)


def _s5_in_proj(x, gain, w_t, tile):
    s = x.shape[0]
    return pl.pallas_call(
        _s5_in_kernel,
        grid=(s // tile,),
        in_specs=[pl.BlockSpec((tile, D_MODEL), lambda i: (i, 0)),
                  _full_spec((1, D_MODEL)),
                  _full_spec((BRANCH, D_MODEL))],
        out_specs=pl.BlockSpec((BRANCH, tile), lambda i: (0, i)),
        out_shape=jax.ShapeDtypeStruct((BRANCH, s), BF16),
        compiler_params=pltpu.CompilerParams(dimension_semantics=("arbitrary",),
                                             vmem_limit_bytes=VMEM_LIMIT),
        name="s5_in_proj",
    )(x, gain, w_t)


def _cmul(xr, xi, yr, yi):
    return xr * yr - xi * yi, xr * yi + xi * yr


def _s5_kernel(u_ref, are_c_ref, aim_c_ref, are_r_ref, aim_r_ref, ldt_ref,
               b_re_ref, b_im_ref, bt_re_ref, bt_im_ref, c_re_ref, c_im_ref, ct_re_ref, ct_im_ref, d_ref,
               o_ref, k_sc, ta_sc, tb_sc, acc_sc, win_sc, m_sc):
    nch, nblk, blk = u_ref.shape
    half = S5_STATE
    blk_f = float(blk)
    dt = jnp.exp(ldt_ref[...])

    def zoh_factor(are, aim, abr, abi):
        den = are * are + aim * aim
        zr = abr - 1.0
        return (zr * are + abi * aim) / den, (abi * are - zr * aim) / den

    are_r, aim_r = are_r_ref[...], aim_r_ref[...]
    rho_r, th_r = dt * are_r, dt * aim_r
    mag_r = jnp.exp(rho_r)
    fr_r, fi_r = zoh_factor(are_r, aim_r, mag_r * jnp.cos(th_r), mag_r * jnp.sin(th_r))
    bbr, bbi = _cmul(fr_r, fi_r, bt_re_ref[...], bt_im_ref[...])

    are_c, aim_c = are_c_ref[...], aim_c_ref[...]
    rho_c, th_c = dt * are_c, dt * aim_c
    lag_l = lax.broadcasted_iota(jnp.int32, (half, blk), 1).astype(F32)
    pmag = jnp.exp(rho_c * lag_l)
    p_re, p_im = pmag * jnp.cos(th_c * lag_l), pmag * jnp.sin(th_c * lag_l)
    abr_c, abi_c = p_re[:, 1:2], p_im[:, 1:2]
    fr_c, fi_c = zoh_factor(are_c, aim_c, abr_c, abi_c)
    bcr, bci = _cmul(fr_c, fi_c, b_re_ref[...], b_im_ref[...])

    c_re, c_im = c_re_ref[...], c_im_ref[...]
    cb_rows = []
    for hi in range(nch):
        cbr, cbi = _cmul(c_re, c_im, bbr[hi:hi + 1, :], bbi[hi:hi + 1, :])
        cb_rows.append(jnp.concatenate([cbr, cbi], axis=1))
    cb = jnp.concatenate(cb_rows, axis=0)
    pstack = jnp.concatenate([p_re, -p_im], axis=0)
    kern = jnp.dot(cb, pstack, preferred_element_type=F32, precision=lax.Precision.HIGHEST)
    rr = lax.broadcasted_iota(jnp.int32, (nch * nch, blk), 0)
    ll = lax.broadcasted_iota(jnp.int32, (nch * nch, blk), 1)
    d_col = jnp.concatenate([d_ref[...]] * nch, axis=0)
    diag = ((rr >> 4) == (rr & (nch - 1))) & (ll == 0)
    k_sc[...] = kern + jnp.where(diag, d_col, 0.0)

    lag_s = blk_f - 1.0 - lag_l
    qmag = jnp.exp(rho_c * lag_s)
    q_re, q_im = qmag * jnp.cos(th_c * lag_s), qmag * jnp.sin(th_c * lag_s)
    for hi in range(nch):
        wr, wi = _cmul(q_re, q_im, bcr[:, hi:hi + 1], bci[:, hi:hi + 1])
        win_sc[:, hi * blk:(hi + 1) * blk] = jnp.concatenate([wr, wi], axis=0).astype(BF16)

    p1_re, p1_im = _cmul(p_re, p_im, abr_c, abi_c)
    ct_re, ct_im = ct_re_ref[...], ct_im_ref[...]
    for ho in range(nch):
        mr, mi = _cmul(p1_re, p1_im, ct_re[:, ho:ho + 1], ct_im[:, ho:ho + 1])
        m_sc[:, ho * blk:(ho + 1) * blk] = jnp.concatenate([mr, -mi], axis=0).astype(BF16)

    ucat = jnp.concatenate([u_ref[h] for h in range(nch)], axis=1)

    x = _dot_nt(ucat, win_sc[...])
    wmag = jnp.exp(rho_r * blk_f)
    w_re, w_im = wmag * jnp.cos(th_r * blk_f), wmag * jnp.sin(th_r * blk_f)
    row = lax.broadcasted_iota(jnp.int32, (nblk, 2 * half), 0)

    def shift_rows(a, k):
        return jnp.where(row >= k, pltpu.roll(a, k, 0), 0.0)

    def state_mul(a, wr, wi):
        return (a * jnp.concatenate([wr, wr], axis=1)
                + pltpu.roll(a, half, 1) * jnp.concatenate([-wi, wi], axis=1))

    step = 1
    while step < nblk:
        x = x + state_mul(shift_rows(x, step), w_re, w_im)
        w_re, w_im = _cmul(w_re, w_im, w_re, w_im)
        step *= 2
    acc_sc[...] = _dot(shift_rows(x, 1).astype(BF16), m_sc[...])

    srow = lax.broadcasted_iota(jnp.int32, (blk, blk), 0)
    lcol = lax.broadcasted_iota(jnp.int32, (blk, blk), 1)
    lower = lcol >= srow

    def build(pair, slab):
        for sub in range(2):
            for ho in range(nch):
                krow = k_sc[pl.ds((2 * pair + sub) * nch + ho, 1), :]
                tile = pltpu.roll(jnp.broadcast_to(krow, (blk, blk)), 0, 1, stride=1, stride_axis=0)
                slab[sub * blk:(sub + 1) * blk, ho * blk:(ho + 1) * blk] = (
                    jnp.where(lower, tile, 0.0).astype(BF16))

    def apply(pair, slab):
        u_pair = jnp.concatenate([u_ref[2 * pair], u_ref[2 * pair + 1]], axis=1)
        acc_sc[...] += _dot(u_pair, slab[...])

    npair = nch // 2
    build(0, ta_sc)

    def two_pairs(q, carry):
        build(2 * q + 1, tb_sc)
        apply(2 * q, ta_sc)
        build(jnp.minimum(2 * q + 2, npair - 1), ta_sc)
        apply(2 * q + 1, tb_sc)
        return carry

    lax.fori_loop(0, npair // 2, two_pairs, 0)
    for ho in range(nch):
        o_ref[ho] = acc_sc[:, ho * blk:(ho + 1) * blk]


def _s5_core(u_t, prm):
    s = u_t.shape[1]
    nblk = s // S5_BLOCK
    u4 = u_t.reshape(S5_GROUPS, S5_GROUP_CH, nblk, S5_BLOCK)

    def gspec(*dims):
        zeros = (0,) * len(dims)
        return pl.BlockSpec((None,) + dims, lambda g: (g,) + zeros)

    width = S5_GROUP_CH * S5_BLOCK
    out = pl.pallas_call(
        _s5_kernel,
        grid=(S5_GROUPS,),
        in_specs=[gspec(S5_GROUP_CH, nblk, S5_BLOCK),
                  gspec(S5_STATE, 1), gspec(S5_STATE, 1), gspec(1, S5_STATE), gspec(1, S5_STATE),
                  gspec(1, 1),
                  gspec(S5_STATE, S5_GROUP_CH), gspec(S5_STATE, S5_GROUP_CH),
                  gspec(S5_GROUP_CH, S5_STATE), gspec(S5_GROUP_CH, S5_STATE),
                  gspec(S5_GROUP_CH, S5_STATE), gspec(S5_GROUP_CH, S5_STATE),
                  gspec(S5_STATE, S5_GROUP_CH), gspec(S5_STATE, S5_GROUP_CH),
                  gspec(S5_GROUP_CH, 1)],
        out_specs=gspec(S5_GROUP_CH, nblk, S5_BLOCK),
        out_shape=jax.ShapeDtypeStruct(u4.shape, F32),
        scratch_shapes=[pltpu.VMEM((S5_GROUP_CH * S5_GROUP_CH, S5_BLOCK), F32),
                        pltpu.VMEM((2 * S5_BLOCK, width), BF16),
                        pltpu.VMEM((2 * S5_BLOCK, width), BF16),
                        pltpu.VMEM((nblk, width), F32),
                        pltpu.VMEM((2 * S5_STATE, width), BF16),
                        pltpu.VMEM((2 * S5_STATE, width), BF16)],
        compiler_params=pltpu.CompilerParams(dimension_semantics=("arbitrary",),
                                             vmem_limit_bytes=VMEM_LIMIT),
        name="s5_core",
    )(u4, *prm)
    return out.reshape(BRANCH, s)


def _s5_params(a_re, a_im, log_dt, b_re, b_im, c_re, c_im, d):
    g = S5_GROUPS
    return (a_re.reshape(g, S5_STATE, 1), a_im.reshape(g, S5_STATE, 1),
            a_re.reshape(g, 1, S5_STATE), a_im.reshape(g, 1, S5_STATE),
            log_dt.reshape(g, 1, 1),
            b_re, b_im,
            jnp.swapaxes(b_re, 1, 2), jnp.swapaxes(b_im, 1, 2),
            c_re, c_im,
            jnp.swapaxes(c_re, 1, 2), jnp.swapaxes(c_im, 1, 2),
            d.reshape(g, S5_GROUP_CH, 1))


_C_GQ, _C_GK, _C_GV, _C_GR = 0, 512, 1024, 1536
_C_MQ, _C_MK, _C_MV, _C_MO = 2048, 2560, 3072, 3584
_C_G0 = 4096
_C_LR = _C_G0 + 3 * D_MODEL
_C_IF = _C_LR + LANES
_W_COLS = _C_IF + LANES

GROUP = 4 * CHUNK
_LOG2_CHUNK = CHUNK.bit_length() - 1
_LOG2_HEAD_DIM = HEAD_DIM.bit_length() - 1
_GATE_SLICE = 256
_N_GATE_SLICES = 3 * D_MODEL // _GATE_SLICE


def _group_masks():
    r = lax.broadcasted_iota(jnp.int32, (GROUP, GROUP), 0)
    c = lax.broadcasted_iota(jnp.int32, (GROUP, GROUP), 1)
    causal = ((r >> _LOG2_CHUNK) == (c >> _LOG2_CHUNK)) & (r >= c)
    rr = lax.broadcasted_iota(jnp.int32, (4 * HEAD_DIM, GROUP), 0)
    cc = lax.broadcasted_iota(jnp.int32, (4 * HEAD_DIM, GROUP), 1)
    stacked = (rr >> _LOG2_HEAD_DIM) == (cc >> _LOG2_CHUNK)
    return causal, stacked


def _cumsum_chunks(tri, a):
    hi = a.astype(BF16)
    mid = (a - hi.astype(F32)).astype(BF16)
    return _dot(tri, hi) + _dot(tri, mid)


def _chunk_rows(c):
    return slice(c * CHUNK, (c + 1) * CHUNK)


def _per_chunk_sums(x_t, stacked, rhs):
    lhs = jnp.where(stacked, jnp.concatenate([x_t] * 4, axis=0), 0.0).astype(BF16)
    return _dot(lhs, rhs)


def _gla_group(q, k, v, la, states, causal, tri, stacked, filler):
    hd = HEAD_DIM
    b = _cumsum_chunks(tri, la)
    last = [b[c * CHUNK + CHUNK - 1:(c + 1) * CHUNK, :] for c in range(4)]
    b_last = jnp.concatenate([jnp.broadcast_to(r, (CHUNK, BRANCH)) for r in last], axis=0)
    q_dec = (q * (hd ** -0.5) * jnp.exp(b)).astype(BF16)
    k_inv = (k * jnp.exp(-b)).astype(BF16)
    k_end = (k * jnp.exp(b_last - b)).astype(BF16)
    vb = v.astype(BF16)
    decay = [jnp.exp(r) for r in last]
    outs, new_states = [], []
    for h in range(HEADS):
        sl = slice(h * hd, (h + 1) * hd)
        filler()
        scores = jnp.where(causal, _dot_nt(q_dec[:, sl], k_inv[:, sl]), 0.0).astype(BF16)
        intra = _dot(scores, vb[:, sl])
        kv = _per_chunk_sums(v[:, sl].T, stacked, k_end[:, sl])
        st = states[h]
        inter = []
        for c in range(4):
            inter.append(_dot_nt(q_dec[_chunk_rows(c), sl], st.astype(BF16)))
            st = st * decay[c][:, sl] + kv[c * hd:(c + 1) * hd, :]
        new_states.append(st)
        outs.append(intra + jnp.concatenate(inter, axis=0))
    return jnp.concatenate(outs, axis=1), new_states


def _mlstm_group(q, k, v, gates, states, m_run, causal, tri, stacked, filler):
    hd = HEAD_DIM
    cs = _cumsum_chunks(tri, gates)
    log_i = pltpu.roll(gates, HEADS, 1)
    w_rows, m_prev_rows, scale = [], [], []
    for c in range(4):
        rows = _chunk_rows(c)
        b_last = cs[c * CHUNK + CHUNK - 1:(c + 1) * CHUNK, :]
        g_end = b_last - cs[rows, :] + log_i[rows, :]
        m_new = jnp.maximum(b_last + m_run, jnp.max(g_end, axis=0, keepdims=True))
        w_rows.append(jnp.exp(g_end - m_new))
        scale.append(jnp.exp(b_last + m_run - m_new))
        m_prev_rows.append(jnp.broadcast_to(m_run, (CHUNK, LANES)))
        m_run = m_new
    w_all = jnp.concatenate(w_rows, axis=0)
    m_inter = cs + jnp.concatenate(m_prev_rows, axis=0)
    lane = lax.broadcasted_iota(jnp.int32, (GROUP, LANES), 1)
    z_t = jnp.where(lane < HEADS, gates, cs).T
    ones = jnp.ones((GROUP, hd), BF16)
    outs, new_states = [], []
    for h in range(HEADS):
        sl = slice(h * hd, (h + 1) * hd)
        col = HEADS + h
        filler()
        log_w = jnp.where(causal, cs[:, col:col + 1] - z_t[col:col + 1, :] + z_t[h:h + 1, :], -jnp.inf)
        m_int = m_inter[:, col:col + 1]
        m = jnp.maximum(m_int, jnp.max(log_w, axis=-1, keepdims=True))
        s_inter = jnp.exp(m_int - m)
        qb = q[:, sl].astype(BF16)
        kh = k[:, sl]
        v_aug = jnp.concatenate([v[:, sl].astype(BF16), ones], axis=1)
        qk = (_dot_nt(qb, kh.astype(BF16)) * jnp.exp(log_w - m)).astype(BF16)
        intra = _dot(qk, v_aug)
        kw_t = (kh * w_all[:, col:col + 1]).T
        kv = _per_chunk_sums(kw_t, stacked, v_aug)
        st = states[h]
        inter = []
        for c in range(4):
            inter.append(_dot(qb[_chunk_rows(c), :], st.astype(BF16)))
            st = scale[c][:, col:col + 1] * st + kv[c * hd:(c + 1) * hd, :]
        new_states.append(st)
        tot = s_inter * jnp.concatenate(inter, axis=0) + intra
        outs.append(tot[:, :hd] / jnp.maximum(jnp.abs(tot[:, hd:]), jnp.exp(-m)))
    return jnp.concatenate(outs, axis=1), new_states, m_run


def _mixer_kernel(x_ref, yst_ref, gpre_ref, gpost_ref, win_ref,
                  wgk_ref, bgk_ref, gnorm_ref, wgp_ref,
                  wglu_ref, bglu_ref, wsp_ref,
                  cw_ref, cb_ref, bif_ref, wmp_ref, wout_ref,
                  o_ref,
                  gst_sc, mc_sc, mm_sc, carry_sc):
    tm = x_ref.shape[0]
    ngroup = tm // GROUP
    hd = HEAD_DIM

    @pl.when(pl.program_id(0) == 0)
    def _init():
        gst_sc[...] = jnp.zeros_like(gst_sc)
        mc_sc[...] = jnp.zeros_like(mc_sc)
        mm_sc[...] = jnp.zeros_like(mm_sc)
        carry_sc[...] = jnp.zeros_like(carry_sc)

    x = x_ref[...]
    u = _rms_norm(x, gpre_ref[...]).astype(BF16)

    def proj(c0, width):
        return _dot(u, win_ref[:, c0:c0 + width])

    causal, stacked = _group_masks()
    tri = causal.astype(BF16)

    done, queue = {}, []

    def filler(n=1):
        for _ in range(n):
            if queue:
                name, fn = queue.pop(0)
                done[name] = fn()

    def need(name):
        while name not in done:
            filler()
        return done[name]

    def s5_tail():
        ys = _gelu_tanh(yst_ref[...].T)
        glu = _sigmoid(_dot(ys.astype(BF16), wglu_ref[...]) + bglu_ref[...])
        return _dot((ys * glu).astype(BF16), wsp_ref[...])

    queue += [("mq", lambda: proj(_C_MQ, BRANCH)), ("mk", lambda: proj(_C_MK, BRANCH)),
              ("mv", lambda: proj(_C_MV, BRANCH)),
              ("pre_if", lambda: proj(_C_IF, LANES) + bif_ref[...]),
              ("mo", lambda: _sigmoid(proj(_C_MO, BRANCH))),
              ("g_r", lambda: _silu(proj(_C_GR, BRANCH)))]
    queue += [("gate%d" % i, functools.partial(
        lambda c0: _sigmoid(proj(c0, _GATE_SLICE)), _C_G0 + i * _GATE_SLICE))
        for i in range(_N_GATE_SLICES)]
    queue.insert(8, ("y_s5", s5_tail))
    fill_gla = -(-8 // (HEADS * ngroup))
    fill_ml = -(-(len(queue) + 1 - 8) // (HEADS * ngroup))

    gq, gk, gv = proj(_C_GQ, BRANCH), proj(_C_GK, BRANCH), proj(_C_GV, BRANCH)
    g_lr = proj(_C_LR, LANES).astype(BF16)
    log_a = _log_sigmoid(_dot(g_lr, wgk_ref[...]) + bgk_ref[...]) / GLA_GATE_NORM
    states = [gst_sc[h] for h in range(HEADS)]
    o_groups = []
    for g in range(ngroup):
        r = slice(g * GROUP, (g + 1) * GROUP)
        o_g, states = _gla_group(gq[r], gk[r], gv[r], log_a[r], states, causal, tri, stacked,
                                 functools.partial(filler, fill_gla))
        o_groups.append(o_g)
    for h in range(HEADS):
        gst_sc[h] = states[h]
    o = jnp.concatenate(o_groups, axis=0)
    gnorm = gnorm_ref[...]
    o_heads = [_rms_norm(o[:, h * hd:(h + 1) * hd], gnorm) for h in range(HEADS)]
    o_gated = (jnp.concatenate(o_heads, axis=1) * need("g_r")).astype(BF16)
    queue.insert(0, ("y_gla", lambda: _dot(o_gated, wgp_ref[...])))

    qk_pre = jnp.concatenate([need("mq"), need("mk")], axis=1)
    prev = carry_sc[...]
    carry_sc[...] = qk_pre[tm - SUBLANES:, :]
    conv = _causal_conv_rows(qk_pre, prev, cw_ref[...], cb_ref[...])
    qk_act = _silu(conv)
    mq = qk_act[:, 0:BRANCH]
    mk = qk_act[:, BRANCH:] * (hd ** -0.5)
    mv = need("mv")
    pre_if = need("pre_if")
    lane = lax.broadcasted_iota(jnp.int32, (tm, LANES), 1)
    gates = jnp.where(lane < HEADS, pre_if, _log_sigmoid(pre_if))
    states = [mc_sc[h] for h in range(HEADS)]
    m_run = mm_sc[0:1, :]
    h_groups = []
    for g in range(ngroup):
        r = slice(g * GROUP, (g + 1) * GROUP)
        h_g, states, m_run = _mlstm_group(mq[r], mk[r], mv[r], gates[r], states, m_run,
                                          causal, tri, stacked, functools.partial(filler, fill_ml))
        h_groups.append(h_g)
    for h in range(HEADS):
        mc_sc[h] = states[h]
    mm_sc[...] = jnp.broadcast_to(m_run, mm_sc.shape)
    hid = (need("mo") * jnp.concatenate(h_groups, axis=0)).astype(BF16)
    y_ml = _dot(hid, wmp_ref[...])

    per_branch = D_MODEL // _GATE_SLICE
    branch_gate = [jnp.concatenate([need("gate%d" % (b * per_branch + i)) for i in range(per_branch)],
                                   axis=1) for b in range(3)]
    merged = branch_gate[0] * need("y_gla") + branch_gate[1] * need("y_s5") + branch_gate[2] * y_ml
    mix = _dot(merged.astype(BF16), wout_ref[...])
    o_ref[...] = x + _rms_norm(mix, gpost_ref[...])


def _mixer(x, ys_t, w, layer, tile):
    s = x.shape[0]
    weights = (w["gpre"], w["gpost"], w["win"], w["wgk"], w["bgk"], w["gnorm"], w["wgp"],
               w["wglu"], w["bglu"], w["wsp"], w["cw"], w["cb"], w["bif"], w["wmp"], w["wout"])
    return pl.pallas_call(
        _mixer_kernel,
        grid=(s // tile,),
        in_specs=[pl.BlockSpec((tile, D_MODEL), lambda i: (i, 0)),
                  pl.BlockSpec((BRANCH, tile), lambda i: (0, i))]
                 + [_weight_spec(a, layer) for a in weights],
        out_specs=pl.BlockSpec((tile, D_MODEL), lambda i: (i, 0)),
        out_shape=jax.ShapeDtypeStruct((s, D_MODEL), F32),
        scratch_shapes=[pltpu.VMEM((HEADS, HEAD_DIM, HEAD_DIM), F32),
                        pltpu.VMEM((HEADS, HEAD_DIM, 2 * HEAD_DIM), F32),
                        pltpu.VMEM((SUBLANES, LANES), F32),
                        pltpu.VMEM((SUBLANES, 2 * BRANCH), F32)],
        compiler_params=pltpu.CompilerParams(dimension_semantics=("arbitrary",),
                                             vmem_limit_bytes=VMEM_LIMIT),
        name="mixer",
    )(x, ys_t, *weights)


_FFN_ROW_BLOCK = 256


def _ffn_kernel(x_ref, gpre_ref, gpost_ref, wup_ref, wgate_ref, cw_ref, cb_ref, wdown_ref,
                o_ref, a_sc, carry_sc):
    tm = x_ref.shape[0]
    rb = min(_FFN_ROW_BLOCK, tm)

    @pl.when(pl.program_id(0) == 0)
    def _zero_carry():
        carry_sc[...] = jnp.zeros_like(carry_sc)

    a_sc[0:SUBLANES, :] = carry_sc[...]
    cw = cw_ref[...]
    for r in range(tm // rb):
        rows = slice(r * rb, (r + 1) * rb)
        x = x_ref[rows, :]
        u = _rms_norm(x, gpre_ref[...]).astype(BF16)
        a = _dot(u, wup_ref[...])
        gate = _dot(u, wgate_ref[...])
        a_sc[SUBLANES + r * rb:SUBLANES + (r + 1) * rb, :] = a
        conv = cb_ref[...] + cw[FFN_CONV - 1:FFN_CONV, :] * a
        for j in range(1, FFN_CONV):
            lo = SUBLANES + r * rb - j
            conv = conv + cw[FFN_CONV - 1 - j:FFN_CONV - j, :] * a_sc[lo:lo + rb, :]
        h = (_gelu_tanh(conv) * gate).astype(BF16)
        o_ref[rows, :] = x + _rms_norm(_dot(h, wdown_ref[...]), gpost_ref[...])

    carry_sc[...] = a_sc[tm:tm + SUBLANES, :]


def _ffn(x, w, layer, tile):
    s = x.shape[0]
    weights = (w["gpre"], w["gpost"], w["wup"], w["wgate"], w["cw"], w["cb"], w["wdown"])
    return pl.pallas_call(
        _ffn_kernel,
        grid=(s // tile,),
        in_specs=[pl.BlockSpec((tile, D_MODEL), lambda i: (i, 0))]
                 + [_weight_spec(a, layer) for a in weights],
        out_specs=pl.BlockSpec((tile, D_MODEL), lambda i: (i, 0)),
        out_shape=jax.ShapeDtypeStruct((s, D_MODEL), F32),
        scratch_shapes=[pltpu.VMEM((tile + SUBLANES, D_FF), F32),
                        pltpu.VMEM((SUBLANES, D_FF), F32)],
        compiler_params=pltpu.CompilerParams(dimension_semantics=("arbitrary",),
                                             vmem_limit_bytes=VMEM_LIMIT),
        name="conv_ffn",
    )(x, *weights)


def _pad_cols(a, width):
    return jnp.pad(a, ((0, 0), (0, width - a.shape[1])))


_PACK_ROWS = 128


def _pack_w_in_kernel(w_ref, o_ref, s5_ref):
    w = w_ref[...]
    rows = w.shape[0]

    def padded(a):
        return jnp.concatenate([a, jnp.zeros((rows, LANES - a.shape[1]), a.dtype)], axis=1)

    packed = jnp.concatenate(
        [w[:, 0:1536], w[:, 1552:2064], w[:, 2576:4112], w[:, 4120:7704],
         padded(w[:, 1536:1552]), padded(w[:, 4112:4120])], axis=1)
    o_ref[...] = packed.astype(BF16)
    s5_ref[...] = w[:, 2064:2576].T.astype(BF16)


def _pack_w_in(w_in, layer):
    _, d, cols = w_in.shape
    return pl.pallas_call(
        _pack_w_in_kernel,
        grid=(d // _PACK_ROWS,),
        in_specs=[pl.BlockSpec((None, _PACK_ROWS, cols), lambda i: (layer, i, 0))],
        out_specs=[pl.BlockSpec((_PACK_ROWS, _W_COLS), lambda i: (i, 0)),
                   pl.BlockSpec((BRANCH, _PACK_ROWS), lambda i: (0, i))],
        out_shape=[jax.ShapeDtypeStruct((d, _W_COLS), BF16),
                   jax.ShapeDtypeStruct((BRANCH, d), BF16)],
        compiler_params=pltpu.CompilerParams(dimension_semantics=("arbitrary",),
                                             vmem_limit_bytes=VMEM_LIMIT),
        name="pack_w_in",
    )(w_in)


def _pack_layer(l, p):
    win, s5_w = _pack_w_in(p["w_in"], l)
    row = lambda a: a.reshape(1, -1)
    mixer = dict(
        gpre=row(p["norm_mix_pre"][l]), gpost=row(p["norm_mix_post"][l]), win=win,
        wgk=jnp.pad(p["gla_w_gk"][l], ((0, LANES - GLA_RANK), (0, 0))).astype(BF16),
        bgk=row(p["gla_b_gk"][l]), gnorm=row(p["gla_norm"][l]),
        wgp=p["gla_w_proj"].astype(BF16),
        wglu=p["s5_w_glu"].astype(BF16), bglu=row(p["s5_b_glu"][l]),
        wsp=p["s5_w_proj"].astype(BF16),
        cw=p["ml_conv_w"][l], cb=row(p["ml_conv_b"][l]),
        bif=_pad_cols(row(jnp.concatenate([p["ml_b_i"][l], p["ml_b_f"][l]])), LANES),
        wmp=p["ml_w_proj"].astype(BF16), wout=p["w_out"].astype(BF16))
    s5_prm = _s5_params(p["s5_a_re"][l], p["s5_a_im"][l], p["s5_log_dt"][l], p["s5_b_re"][l],
                        p["s5_b_im"][l], p["s5_c_re"][l], p["s5_c_im"][l], p["s5_d"][l])
    ffn = dict(gpre=row(p["norm_ffn_pre"][l]), gpost=row(p["norm_ffn_post"][l]),
               wup=p["ffn_w_up"].astype(BF16), wgate=p["ffn_w_gate"].astype(BF16),
               cw=p["ffn_conv_w"][l], cb=row(p["ffn_conv_b"][l]),
               wdown=p["ffn_w_down"].astype(BF16))
    return mixer, s5_w, s5_prm, ffn


def _tiles(s):
    return min(1024, s), min(512, s), min(512, s)


def kernel(x, norm_mix_pre, norm_mix_post, norm_ffn_pre, norm_ffn_post, w_in, gla_w_gk, gla_b_gk, gla_norm, gla_w_proj, s5_a_re, s5_a_im, s5_log_dt, s5_b_re, s5_b_im, s5_c_re, s5_c_im, s5_d, s5_w_glu, s5_b_glu, s5_w_proj, ml_conv_w, ml_conv_b, ml_b_i, ml_b_f, ml_w_proj, w_out, ffn_w_up, ffn_w_gate, ffn_conv_w, ffn_conv_b, ffn_w_down):
    p = dict(norm_mix_pre=norm_mix_pre, norm_mix_post=norm_mix_post, norm_ffn_pre=norm_ffn_pre,
             norm_ffn_post=norm_ffn_post, w_in=w_in, gla_w_gk=gla_w_gk, gla_b_gk=gla_b_gk,
             gla_norm=gla_norm, gla_w_proj=gla_w_proj, s5_a_re=s5_a_re, s5_a_im=s5_a_im,
             s5_log_dt=s5_log_dt, s5_b_re=s5_b_re, s5_b_im=s5_b_im, s5_c_re=s5_c_re,
             s5_c_im=s5_c_im, s5_d=s5_d, s5_w_glu=s5_w_glu, s5_b_glu=s5_b_glu, s5_w_proj=s5_w_proj,
             ml_conv_w=ml_conv_w, ml_conv_b=ml_conv_b, ml_b_i=ml_b_i, ml_b_f=ml_b_f,
             ml_w_proj=ml_w_proj, w_out=w_out, ffn_w_up=ffn_w_up, ffn_w_gate=ffn_w_gate,
             ffn_conv_w=ffn_conv_w, ffn_conv_b=ffn_conv_b, ffn_w_down=ffn_w_down)
    bsz, s, _ = x.shape
    assert bsz == 1 and s % (S5_BLOCK * SUBLANES) == 0
    t_s5, t_mix, t_ffn = _tiles(s)
    h = x[0]
    for l in range(w_in.shape[0]):
        mixer_w, s5_w, s5_prm, ffn_w = _pack_layer(l, p)
        u_t = _s5_in_proj(h, mixer_w["gpre"], s5_w, t_s5)
        y_t = _s5_core(u_t, s5_prm)
        h = _mixer(h, y_t, mixer_w, l, t_mix)
        h = _ffn(h, ffn_w, l, t_ffn)
    return h[None]
```
